```python
import math
import jax, jax.numpy as jnp
from jax import lax
import numpy as np

D_MODEL = 1024
BATCH = 16
SEQ = 2048
DEPTH = 1
DEC_BATCH = 128
DEC_SEQ = 4
PAST_LEN = 8192
PAGE_SIZE = 128

POOL_WIDTH = D_MODEL // 2
POOL_WINDOWS = (2, 4, 8, 16)
N_POOL_GROUPS = len(POOL_WINDOWS)
POOL_GROUP = POOL_WIDTH // N_POOL_GROUPS
POOL_STATE = max(POOL_WINDOWS) - 1
N_HEADS = 4
HEAD_DIM = 64
V_DIM = 2 * HEAD_DIM
QK_WIDTH = N_HEADS * 2 * HEAD_DIM
V_WIDTH = N_HEADS * V_DIM
ATTN_WIDTH = V_WIDTH
N_BUCKETS = 32
MAX_DISTANCE = 128
Q_BLOCK = 128
IN_SPLITS = (POOL_WIDTH, POOL_WIDTH + QK_WIDTH, POOL_WIDTH + 2 * QK_WIDTH,
             POOL_WIDTH + 2 * QK_WIDTH + V_WIDTH, POOL_WIDTH + 2 * QK_WIDTH + V_WIDTH + D_MODEL)
IN_COLS = POOL_WIDTH + 2 * QK_WIDTH + V_WIDTH + 2 * D_MODEL
PEER_HEADS = 8
PEER_TOPK = 16
N_KEYS = 128
N_EXPERTS = N_KEYS * N_KEYS
PEER_QDIM = 256
PEER_HALF = PEER_QDIM // 2
PEER_BLOCK = 128
EPS = 1e-6
NEG_INF = -1e30
F32 = jnp.float32

kernel_name = 'hybrid_pool_diffattn_peer_step'


def rmsnorm(x, g):
    xf = x.astype(F32)
    y = xf * lax.rsqrt(jnp.mean(xf * xf, axis=-1, keepdims=True) + EPS)
    return (y * g.astype(F32)).astype(x.dtype)


def mixer_inputs(xn, w_in):
    z = xn @ w_in
    b, l, _ = z.shape
    u_pool, q, k, v, g_pool, g_attn = jnp.split(z, IN_SPLITS, axis=-1)
    q = q.reshape(b, l, N_HEADS, 2, HEAD_DIM)
    k = k.reshape(b, l, N_HEADS, 2, HEAD_DIM)
    v = v.reshape(b, l, N_HEADS, V_DIM)
    return u_pool, q, k, v, jax.nn.sigmoid(g_pool), jax.nn.sigmoid(g_attn)


def multi_scale_pool(u, pos):
    b, l, _ = u.shape
    ug = u.astype(F32).reshape(b, l, N_POOL_GROUPS, POOL_GROUP)
    cs = jnp.cumsum(ug, axis=1)
    outs = []
    for gi, w in enumerate(POOL_WINDOWS):
        c = cs[:, :, gi]
        prev = jnp.pad(c, ((0, 0), (w, 0), (0, 0)))[:, :l]
        cnt = jnp.minimum(pos + 1, w).astype(F32)
        outs.append((c - prev) / cnt[None, :, None])
    return jnp.stack(outs, axis=2) - ug


def pool_project(mixed, grp_w, scale, dtype):
    b, l = mixed.shape[:2]
    y = jnp.einsum('blgc,gcd->blgd', mixed.astype(dtype), grp_w)
    return y.reshape(b, l, POOL_WIDTH) * scale


def t5_bucket(q_pos, k_pos):
    rel = k_pos[None, :] - q_pos[:, None]
    n = jnp.maximum(-rel, 0)
    max_exact = N_BUCKETS // 2
    nf = jnp.maximum(n, 1).astype(F32)
    large = max_exact + (jnp.log(nf / max_exact) / math.log(MAX_DISTANCE / max_exact)
                         * (N_BUCKETS - max_exact)).astype(jnp.int32)
    large = jnp.minimum(large, N_BUCKETS - 1)
    return jnp.where(n < max_exact, n, large)


def diff_logits(q, k, q_pos, k_pos, rel_bias):
    s = jnp.einsum('bqhmd,bkhmd->bhmqk', q, k, preferred_element_type=F32) * (HEAD_DIM ** -0.5)
    bias = rel_bias.astype(F32)[t5_bucket(q_pos, k_pos)].transpose(2, 0, 1)
    s = s + bias[None, :, None]
    mask = k_pos[None, :] <= q_pos[:, None]
    return jnp.where(mask, s, NEG_INF)


def diff_weights(s, lam):
    p = jax.nn.softmax(s, axis=-1)
    return p[:, :, 0] - lam * p[:, :, 1]


def prompt_diff_attention(q, k, v, rel_bias, lam):
    b, s_len = q.shape[:2]
    nblk = s_len // Q_BLOCK
    pos = jnp.arange(s_len, dtype=jnp.int32)
    qb = q.reshape(b, nblk, Q_BLOCK, N_HEADS, 2, HEAD_DIM).swapaxes(0, 1)
    pb = pos.reshape(nblk, Q_BLOCK)

    def block(args):
        qi, pi = args
        a = diff_weights(diff_logits(qi, k, pi, pos, rel_bias), lam)
        return jnp.einsum('bhqk,bkhe->bqhe', a.astype(v.dtype), v)

    o = lax.map(block, (qb, pb))
    return o.swapaxes(0, 1).reshape(b, s_len, N_HEADS, V_DIM)


def sample_diff_attention(q, k, v, k_past, v_past, rel_bias, lam):
    t = q.shape[1]
    p = k_past.shape[1]
    pos_past = jnp.arange(p, dtype=jnp.int32)
    pos_new = p + jnp.arange(t, dtype=jnp.int32)
    s = jnp.concatenate([diff_logits(q, k_past, pos_new, pos_past, rel_bias),
                         diff_logits(q, k, pos_new, pos_new, rel_bias)], axis=-1)
    a = diff_weights(s, lam).astype(v.dtype)
    return (jnp.einsum('bhqk,bkhe->bqhe', a[..., :p], v_past)
            + jnp.einsum('bhqk,bkhe->bqhe', a[..., p:], v))


def diff_head_out(o, subln_g, lam_init):
    b, l = o.shape[:2]
    o = rmsnorm(o, subln_g) * (1.0 - lam_init)
    return o.reshape(b, l, ATTN_WIDTH)


def merge(x, pool_o, attn_o, g_pool, g_attn, w_bp, w_ba, w_out):
    m = g_pool * (pool_o.astype(x.dtype) @ w_bp) + g_attn * (attn_o @ w_ba)
    return x + m @ w_out


def peer(xn, wq, subkeys, u_tab, v_tab):
    b, l, d = xn.shape
    t = xn.reshape(-1, d)
    n_tok = t.shape[0]
    nblk = -(-n_tok // PEER_BLOCK)
    t = jnp.pad(t, ((0, nblk * PEER_BLOCK - n_tok), (0, 0))).reshape(nblk, PEER_BLOCK, d)

    def block(tb):
        q = (tb @ wq).reshape(PEER_BLOCK, PEER_HEADS, 2, PEER_HALF)
        s = jnp.einsum('thpc,hpnc->thpn', q, subkeys, preferred_element_type=F32)
        sv, si = lax.top_k(s, PEER_TOPK)
        cand = (sv[:, :, 0, :, None] + sv[:, :, 1, None, :]).reshape(PEER_BLOCK, PEER_HEADS, -1)
        cid = (si[:, :, 0, :, None] * N_KEYS + si[:, :, 1, None, :]).reshape(PEER_BLOCK, PEER_HEADS, -1)
        top_s, top_j = lax.top_k(cand, PEER_TOPK)
        eid = jnp.take_along_axis(cid, top_j, axis=-1)
        g = jax.nn.softmax(top_s, axis=-1)
        u = jnp.take(u_tab, eid, axis=0)
        act = jax.nn.gelu(jnp.einsum('thkd,td->thk', u, tb, preferred_element_type=F32), approximate=False)
        w = (g * act).astype(v_tab.dtype)
        return jnp.einsum('thk,thkd->td', w, jnp.take(v_tab, eid, axis=0))

    out = lax.map(block, t).reshape(-1, d)[:n_tok]
    return out.reshape(b, l, d).astype(xn.dtype)


def setup_inputs(seed: int = 0) -> dict:
    key = jax.random.key(seed)
    ks = jax.random.split(key, 26)
    n_pages = PAST_LEN // PAGE_SIZE
    n_used = DEC_BATCH * n_pages
    n_phys = n_used + max(n_used // 4, 1)
    nrm = lambda k, shape, s: jax.random.normal(k, shape, F32) * s
    page_table = jax.random.permutation(ks[5], n_phys)[:n_used].reshape(DEC_BATCH, n_pages).astype(jnp.int32)
    return {
        'x_prompt': nrm(ks[0], (BATCH, SEQ, D_MODEL), 1.0),
        'x_sample': nrm(ks[1], (DEC_BATCH, DEC_SEQ, D_MODEL), 1.0),
        'cache_k': nrm(ks[2], (DEPTH, n_phys, PAGE_SIZE, N_HEADS, 2 * HEAD_DIM), 1.0),
        'cache_v': nrm(ks[3], (DEPTH, n_phys, PAGE_SIZE, N_HEADS, V_DIM), 1.0),
        'state_pool': nrm(ks[4], (DEPTH, DEC_BATCH, POOL_STATE, POOL_WIDTH), 1.0),
        'page_table': page_table,
        'norm_mix_g': 1.0 + nrm(ks[6], (DEPTH, D_MODEL), 0.1),
        'w_in': nrm(ks[7], (DEPTH, D_MODEL, IN_COLS), D_MODEL ** -0.5),
        'pool_group_w': nrm(ks[8], (DEPTH, N_POOL_GROUPS, POOL_GROUP, POOL_GROUP), POOL_GROUP ** -0.5),
        'pool_scale': 1.0 + nrm(ks[9], (DEPTH, POOL_WIDTH), 0.1),
        'lambda_q1': nrm(ks[10], (DEPTH, HEAD_DIM), 0.1),
        'lambda_k1': nrm(ks[11], (DEPTH, HEAD_DIM), 0.1),
        'lambda_q2': nrm(ks[12], (DEPTH, HEAD_DIM), 0.1),
        'lambda_k2': nrm(ks[13], (DEPTH, HEAD_DIM), 0.1),
        'subln_g': 1.0 + nrm(ks[14], (DEPTH, V_DIM), 0.1),
        'rel_bias': nrm(ks[15], (N_BUCKETS, N_HEADS), 0.5),
        'w_branch_pool': nrm(ks[16], (DEPTH, POOL_WIDTH, D_MODEL), POOL_WIDTH ** -0.5),
        'w_branch_attn': nrm(ks[17], (DEPTH, ATTN_WIDTH, D_MODEL), ATTN_WIDTH ** -0.5),
        'w_out': nrm(ks[18], (DEPTH, D_MODEL, D_MODEL), D_MODEL ** -0.5),
        'norm_ffn_g': 1.0 + nrm(ks[19], (DEPTH, D_MODEL), 0.1),
        'peer_wq': nrm(ks[20], (DEPTH, D_MODEL, PEER_HEADS * PEER_QDIM), D_MODEL ** -0.5),
        'peer_subkeys': nrm(ks[21], (DEPTH, PEER_HEADS, 2, N_KEYS, PEER_HALF), PEER_HALF ** -0.5),
        'peer_u': nrm(ks[22], (DEPTH, N_EXPERTS, D_MODEL), D_MODEL ** -0.5),
        'peer_v': nrm(ks[23], (DEPTH, N_EXPERTS, D_MODEL), PEER_HEADS ** -0.5),
        'norm_final_g': 1.0 + nrm(ks[24], (D_MODEL,), 0.1),
    }


def reference(x_prompt, x_sample, cache_k, cache_v, state_pool, page_table,
              norm_mix_g, w_in, pool_group_w, pool_scale,
              lambda_q1, lambda_k1, lambda_q2, lambda_k2, subln_g, rel_bias,
              w_branch_pool, w_branch_attn, w_out, norm_ffn_g,
              peer_wq, peer_subkeys, peer_u, peer_v, norm_final_g):
    b_p, s_p = x_prompt.shape[:2]
    b_s, t_s = x_sample.shape[:2]
    past = page_table.shape[1] * cache_k.shape[2]
    pos_p = jnp.arange(s_p, dtype=jnp.int32)
    pos_pool_s = past - POOL_STATE + jnp.arange(POOL_STATE + t_s, dtype=jnp.int32)
    h_p, h_s = x_prompt, x_sample
    kp_l, vp_l, pp_l, ks_l, vs_l, ps_l = [], [], [], [], [], []
    for l in range(DEPTH):
        lam_init = 0.8 - 0.6 * math.exp(-0.3 * l)
        lam = (jnp.exp(jnp.dot(lambda_q1[l].astype(F32), lambda_k1[l].astype(F32)))
               - jnp.exp(jnp.dot(lambda_q2[l].astype(F32), lambda_k2[l].astype(F32))) + lam_init)

        up, qp, kp, vp, gpp, gap = mixer_inputs(rmsnorm(h_p, norm_mix_g[l]), w_in[l])
        pool_p = pool_project(multi_scale_pool(up, pos_p), pool_group_w[l], pool_scale[l], up.dtype)
        attn_p = diff_head_out(prompt_diff_attention(qp, kp, vp, rel_bias, lam), subln_g[l], lam_init)
        h_p = merge(h_p, pool_p, attn_p, gpp, gap, w_branch_pool[l], w_branch_attn[l], w_out[l])
        h_p = h_p + peer(rmsnorm(h_p, norm_ffn_g[l]), peer_wq[l], peer_subkeys[l], peer_u[l], peer_v[l])
        kp_l.append(kp.reshape(b_p, s_p, N_HEADS, 2 * HEAD_DIM))
        vp_l.append(vp)
        pp_l.append(up[:, -POOL_STATE:])

        us, qs, k_new, v_new, gps, gas = mixer_inputs(rmsnorm(h_s, norm_mix_g[l]), w_in[l])
        u_full = jnp.concatenate([state_pool[l].astype(us.dtype), us], axis=1)
        mixed_s = multi_scale_pool(u_full, pos_pool_s)[:, POOL_STATE:]
        pool_s = pool_project(mixed_s, pool_group_w[l], pool_scale[l], us.dtype)
        k_past = cache_k[l, page_table].reshape(b_s, past, N_HEADS, 2, HEAD_DIM)
        v_past = cache_v[l, page_table].reshape(b_s, past, N_HEADS, V_DIM)
        o_s = sample_diff_attention(qs, k_new, v_new, k_past, v_past, rel_bias, lam)
        attn_s = diff_head_out(o_s, subln_g[l], lam_init)
        h_s = merge(h_s, pool_s, attn_s, gps, gas, w_branch_pool[l], w_branch_attn[l], w_out[l])
        h_s = h_s + peer(rmsnorm(h_s, norm_ffn_g[l]), peer_wq[l], peer_subkeys[l], peer_u[l], peer_v[l])
        ks_l.append(k_new.reshape(b_s, t_s, N_HEADS, 2 * HEAD_DIM))
        vs_l.append(v_new)
        ps_l.append(u_full[:, -POOL_STATE:])

    y_prompt = rmsnorm(h_p, norm_final_g)
    y_sample = rmsnorm(h_s, norm_final_g)
    k_prompt = jnp.stack(kp_l, axis=0)
    v_prompt = jnp.stack(vp_l, axis=0)
    pool_prompt = jnp.stack(pp_l, axis=0)
    k_sample = jnp.stack(ks_l, axis=0)
    v_sample = jnp.stack(vs_l, axis=0)
    pool_sample = jnp.stack(ps_l, axis=0)
    return (y_prompt, y_sample, k_prompt, v_prompt, pool_prompt, k_sample, v_sample, pool_sample)
```

```python
import functools
import math

import jax
import jax.numpy as jnp
from jax import lax
from jax.experimental import pallas as pl
from jax.experimental.pallas import tpu as pltpu

F32 = jnp.float32
BF16 = jnp.bfloat16
I32 = jnp.int32

EPS = 1e-6
NEG_INF = -1e30

LANES = 128
SUBLANES = 8
VMEM_LIMIT_BYTES = 56 * 1024 * 1024

POOL_WINDOWS = (2, 4, 8, 16)
POOL_STATE = max(POOL_WINDOWS) - 1
HIST_ROWS = 16
N_HEADS = 4
HEAD_DIM = 64
V_DIM = 2 * HEAD_DIM
N_BUCKETS = 32
MAX_DISTANCE = 128
PEER_HEADS = 8
PEER_TOPK = 16
N_KEYS = 128
PEER_HALF = 128
N_SLOTS = PEER_HEADS * PEER_TOPK
ROW_WORDS = 4
KEY_SHIFT = 14

TOK_TILE = 256
ATT_BLOCK = 256
PAGES_PER_STEP = 8
GATHER_TILE = 128

_NT = (((1,), (1,)), ((), ()))


def _params(*sem):
    return pltpu.CompilerParams(dimension_semantics=sem, vmem_limit_bytes=VMEM_LIMIT_BYTES)


def _rms(x, g):
    return (x * lax.rsqrt(jnp.mean(x * x, axis=-1, keepdims=True) + EPS)) * g


def _resident(shape):
    nd = len(shape)
    return pl.BlockSpec(shape, lambda *_: (0,) * nd)


def _inproj_body(x_ref, g_ref, w_ref, u_ref, q_ref, k_ref, v_ref, gp_ref, ga_ref, *, splits):
    xb = _rms(x_ref[...], g_ref[...]).astype(BF16)

    def proj(i):
        return jnp.dot(xb, w_ref[:, splits[i]:splits[i + 1]], preferred_element_type=F32)

    u_ref[...] = proj(0)
    q_ref[...] = (proj(1) * (HEAD_DIM ** -0.5)).astype(BF16)
    k_ref[...] = proj(2)
    v_ref[...] = proj(3)
    gp_ref[...] = jax.nn.sigmoid(proj(4))
    ga_ref[...] = jax.nn.sigmoid(proj(5))


def _inproj(x, g, w_in_bf):
    n, d = x.shape
    pw = 512
    splits = (0, pw, 2 * pw, 3 * pw, 4 * pw, 4 * pw + d, 4 * pw + 2 * d)
    tm = TOK_TILE
    row = lambda c: pl.BlockSpec((tm, c), lambda i: (i, 0))
    return pl.pallas_call(
        functools.partial(_inproj_body, splits=splits),
        grid=(n // tm,),
        in_specs=[row(d), _resident((1, d)), _resident(w_in_bf.shape)],
        out_specs=[row(pw), row(pw), row(pw), row(pw), row(d), row(d)],
        out_shape=[jax.ShapeDtypeStruct((n, pw), F32), jax.ShapeDtypeStruct((n, pw), BF16),
                   jax.ShapeDtypeStruct((n, pw), F32), jax.ShapeDtypeStruct((n, pw), F32),
                   jax.ShapeDtypeStruct((n, d), F32), jax.ShapeDtypeStruct((n, d), F32)],
        compiler_params=_params("parallel"),
    )(x, g, w_in_bf)


def _pool_body(hist_ref, u_ref, gw_ref, sc_ref, o_ref, buf, *, length, pos0):
    buf[0:HIST_ROWS, :] = hist_ref[...]
    buf[HIST_ROWS:HIST_ROWS + length, :] = u_ref[...]
    t = lax.broadcasted_iota(I32, (length, 1), 0)
    for gi, w in enumerate(POOL_WINDOWS):
        c0 = gi * LANES
        ug = buf[HIST_ROWS:HIST_ROWS + length, c0:c0 + LANES]
        acc = ug
        for j in range(1, w):
            acc = acc + buf[HIST_ROWS - j:HIST_ROWS - j + length, c0:c0 + LANES]
        cnt = jnp.minimum(t + (pos0 + 1), w).astype(F32)
        mixed = acc / cnt - ug
        y = jnp.dot(mixed.astype(BF16), gw_ref[gi], preferred_element_type=F32)
        o_ref[:, c0:c0 + LANES] = y * sc_ref[:, c0:c0 + LANES]


def _pool(hist, u, gw_bf, scale, pos0):
    b, length, pw = u.shape
    return pl.pallas_call(
        functools.partial(_pool_body, length=length, pos0=pos0),
        grid=(b,),
        in_specs=[pl.BlockSpec((None, HIST_ROWS, pw), lambda i: (i, 0, 0)),
                  pl.BlockSpec((None, length, pw), lambda i: (i, 0, 0)),
                  _resident(gw_bf.shape), _resident((1, pw))],
        out_specs=pl.BlockSpec((None, length, pw), lambda i: (i, 0, 0)),
        out_shape=jax.ShapeDtypeStruct((b, length, pw), F32),
        scratch_shapes=[pltpu.VMEM((HIST_ROWS + length, pw), F32)],
        compiler_params=_params("parallel"),
    )(hist, u, gw_bf, scale)


def _lambda(lq1, lk1, lq2, lk2, lam_init):
    d1 = jnp.sum(lq1[...] * lk1[...], axis=-1, keepdims=True)
    d2 = jnp.sum(lq2[...] * lk2[...], axis=-1, keepdims=True)
    return jnp.exp(d1) - jnp.exp(d2) + lam_init


def _softmax_update(s, vals, m, l, acc):
    mn = jnp.maximum(m, jnp.max(s, axis=-1, keepdims=True))
    alpha = jnp.exp(m - mn)
    p = jnp.exp(s - mn)
    l = alpha * l + jnp.sum(p, axis=-1, keepdims=True)
    pb = p.astype(BF16)
    pv = None
    for (c0, c1), vb in vals:
        term = jnp.dot(pb[:, c0:c1], vb, preferred_element_type=F32)
        pv = term if pv is None else pv + term
    return mn, l, alpha * acc + pv


def _head_norm(o, sg, lam_init):
    return _rms(o, sg) * (1.0 - lam_init)


def _t5_bucket(n):
    max_exact = N_BUCKETS // 2
    nf = jnp.maximum(n, 1).astype(F32)
    large = max_exact + (jnp.log(nf / max_exact) / math.log(MAX_DISTANCE / max_exact)
                         * (N_BUCKETS - max_exact)).astype(I32)
    large = jnp.minimum(large, N_BUCKETS - 1)
    return jnp.where(n < max_exact, n, large)


def _rel_bias(rel_bias, n):
    b = rel_bias.astype(F32)[_t5_bucket(jnp.maximum(n, 0))]
    b = jnp.where((n >= 0)[..., None], b, NEG_INF)
    return jnp.moveaxis(b, -1, 0)


def _pattn_body(q_ref, k_ref, v_ref, bias_ref, lq1, lk1, lq2, lk2, sg_ref, o_ref, kb, vb,
                *, blk, lam_init):
    i = pl.program_id(2)

    @pl.when(i == 0)
    def _():
        kb[...] = k_ref[...].astype(BF16)
        vb[...] = v_ref[...].astype(BF16)

    q = q_ref[...]
    lane = lax.broadcasted_iota(I32, q.shape, 1)
    zero = jnp.zeros_like(q)
    q1 = jnp.where(lane < HEAD_DIM, q, zero)
    q2 = jnp.where(lane < HEAD_DIM, zero, q)

    def step(j, carry):
        m1, l1, a1, m2, l2, a2 = carry
        off = pl.multiple_of(j * blk, blk)
        kc = kb[pl.ds(off, blk), :]
        vc = vb[pl.ds(off, blk), :]
        bt = bias_ref[jnp.minimum(i - j, 2)]
        s1 = lax.dot_general(q1, kc, _NT, preferred_element_type=F32) + bt
        s2 = lax.dot_general(q2, kc, _NT, preferred_element_type=F32) + bt
        vals = [((0, blk), vc)]
        m1, l1, a1 = _softmax_update(s1, vals, m1, l1, a1)
        m2, l2, a2 = _softmax_update(s2, vals, m2, l2, a2)
        return m1, l1, a1, m2, l2, a2

    col = lambda v: jnp.full((blk, 1), v, F32)
    acc0 = jnp.zeros((blk, V_DIM), F32)
    init = (col(-jnp.inf), col(0.0), acc0, col(-jnp.inf), col(0.0), acc0)
    _, l1, a1, _, l2, a2 = lax.fori_loop(0, i + 1, step, init)
    lam = _lambda(lq1, lk1, lq2, lk2, lam_init)
    o = a1 / l1 - lam * (a2 / l2)
    o_ref[...] = _head_norm(o, sg_ref[...], lam_init)


def _prompt_attention(q, k, v, bias_tiles, lams, sg, batch, seq, lam_init):
    blk = ATT_BLOCK
    nq = seq // blk
    hw = V_DIM
    lam_spec = _resident((1, HEAD_DIM))
    return pl.pallas_call(
        functools.partial(_pattn_body, blk=blk, lam_init=lam_init),
        grid=(batch, N_HEADS, nq),
        in_specs=[pl.BlockSpec((blk, hw), lambda b, h, i: (b * nq + i, h)),
                  pl.BlockSpec((seq, hw), lambda b, h, i: (b, h)),
                  pl.BlockSpec((seq, hw), lambda b, h, i: (b, h)),
                  pl.BlockSpec((None, 3, blk, blk), lambda b, h, i: (h, 0, 0, 0)),
                  lam_spec, lam_spec, lam_spec, lam_spec, _resident((1, hw))],
        out_specs=pl.BlockSpec((blk, hw), lambda b, h, i: (b * nq + i, h)),
        out_shape=jax.ShapeDtypeStruct((batch * seq, N_HEADS * hw), F32),
        scratch_shapes=[pltpu.VMEM((seq, hw), BF16), pltpu.VMEM((seq, hw), BF16)],
        compiler_params=_params("parallel", "parallel", "arbitrary"),
    )(q, k, v, bias_tiles, *lams, sg)


def _sattn_body(pt_ref, qbd_ref, bias_ref, bnew_ref, knew_ref, vnew_ref, lq1, lk1, lq2, lk2, sg_ref,
                *rest, pages, t_new, lam_init):
    k_refs = rest[:pages]
    v_refs = rest[pages:2 * pages]
    o_ref, m_s, l_s, acc_s = rest[2 * pages:]
    j = pl.program_id(1)
    psz = k_refs[0].shape[0]

    @pl.when(j == 0)
    def _():
        m_s[...] = jnp.full(m_s.shape, -jnp.inf, F32)
        l_s[...] = jnp.zeros(l_s.shape, F32)
        acc_s[...] = jnp.zeros(acc_s.shape, F32)

    qbd = qbd_ref[...]

    def update(k_blocks, v_blocks, bias):
        s = jnp.concatenate(
            [lax.dot_general(qbd, kb.astype(BF16), _NT, preferred_element_type=F32) for kb in k_blocks],
            axis=1) + bias
        vals = [((p * psz, (p + 1) * psz), vb.astype(BF16)) for p, vb in enumerate(v_blocks)]
        m, l, acc = _softmax_update(s, vals, m_s[...], l_s[...], acc_s[...])
        m_s[...] = m
        l_s[...] = l
        acc_s[...] = acc

    update([r[...] for r in k_refs], [r[...] for r in v_refs], bias_ref[j])

    @pl.when(j == pl.num_programs(1) - 1)
    def _():
        update([knew_ref[...]], [vnew_ref[...]], bnew_ref[...])
        lam = _lambda(lq1, lk1, lq2, lk2, lam_init)
        acc = acc_s[...]
        l = l_s[...]
        half = N_HEADS * t_new
        for h in range(N_HEADS):
            r0, c0 = h * t_new, h * V_DIM
            o1 = acc[r0:r0 + t_new, c0:c0 + V_DIM] / l[r0:r0 + t_new]
            o2 = acc[half + r0:half + r0 + t_new, c0:c0 + V_DIM] / l[half + r0:half + r0 + t_new]
            o_ref[:, c0:c0 + V_DIM] = _head_norm(o1 - lam * o2, sg_ref[...], lam_init)


def _sample_attention(page_table, qbd, bias_past, bias_new, k_new, v_new, cache_k, cache_v, lams, sg,
                      t_new, lam_init):
    b, n_pages = page_table.shape
    pages = PAGES_PER_STEP
    nj = n_pages // pages
    psz, width = cache_k.shape[1:]
    rows = qbd.shape[1]
    lam_spec = _resident((1, HEAD_DIM))

    def page_spec(p):
        return pl.BlockSpec((None, psz, width), lambda i, j, pt: (pt[i, j * pages + p], 0, 0))

    per_batch = lambda r: pl.BlockSpec((None, r, width), lambda i, j, pt: (i, 0, 0))
    grid_spec = pltpu.PrefetchScalarGridSpec(
        num_scalar_prefetch=1,
        grid=(b, nj),
        in_specs=[per_batch(rows), _resident(bias_past.shape), _resident(bias_new.shape),
                  per_batch(psz), per_batch(psz),
                  lam_spec, lam_spec, lam_spec, lam_spec, _resident((1, V_DIM))]
                 + [page_spec(p) for p in range(pages)] + [page_spec(p) for p in range(pages)],
        out_specs=per_batch(t_new),
        scratch_shapes=[pltpu.VMEM((rows, 1), F32), pltpu.VMEM((rows, 1), F32),
                        pltpu.VMEM((rows, width), F32)],
    )
    return pl.pallas_call(
        functools.partial(_sattn_body, pages=pages, t_new=t_new, lam_init=lam_init),
        grid_spec=grid_spec,
        out_shape=jax.ShapeDtypeStruct((b, t_new, width), F32),
        compiler_params=_params("parallel", "arbitrary"),
    )(page_table, qbd, bias_past, bias_new, k_new, v_new, *lams, sg,
      *([cache_k] * pages), *([cache_v] * pages))


def _merge_body(x_ref, po_ref, ao_ref, gp_ref, ga_ref, wbp_ref, wba_ref, wout_ref, h_ref):
    a = jnp.dot(po_ref[...].astype(BF16), wbp_ref[...], preferred_element_type=F32)
    b = jnp.dot(ao_ref[...].astype(BF16), wba_ref[...], preferred_element_type=F32)
    m = gp_ref[...] * a + ga_ref[...] * b
    h_ref[...] = x_ref[...] + jnp.dot(m.astype(BF16), wout_ref[...], preferred_element_type=F32)


def _merge(x, pool_o, attn_o, gp, ga, wbp, wba, wout):
    n, d = x.shape
    tm = TOK_TILE
    row = lambda c: pl.BlockSpec((tm, c), lambda i: (i, 0))
    return pl.pallas_call(
        _merge_body,
        grid=(n // tm,),
        in_specs=[row(d), row(pool_o.shape[1]), row(attn_o.shape[1]), row(d), row(d),
                  _resident(wbp.shape), _resident(wba.shape), _resident(wout.shape)],
        out_specs=row(d),
        out_shape=jax.ShapeDtypeStruct((n, d), F32),
        compiler_params=_params("parallel"),
    )(x, pool_o, attn_o, gp, ga, wbp, wba, wout)


def _extract_topk(x, key, k):
    big = jnp.iinfo(jnp.int32).max
    vals, keys = [], []
    for _ in range(k):
        m = jnp.max(x, axis=0, keepdims=True)
        kmin = jnp.min(jnp.where(x == m, key, big), axis=0, keepdims=True)
        x = jnp.where(key == kmin, -jnp.inf, x)
        vals.append(m)
        keys.append(kmin)
    return jnp.concatenate(vals, axis=0), jnp.concatenate(keys, axis=0)


def _route_body(h_ref, g_ref, wq_ref, sk_ref, xn_ref, eid_ref, gate_ref, q_scr, sv_scr, si_scr, *, tm):
    xn = _rms(h_ref[...], g_ref[...])
    xn_ref[...] = xn
    q = jnp.dot(xn.astype(BF16), wq_ref[...], preferred_element_type=F32)
    n_lists = 2 * PEER_HEADS
    for l in range(n_lists):
        q_scr[l] = q[:, l * PEER_HALF:(l + 1) * PEER_HALF].astype(BF16)
    groups = tm // LANES
    key_iota = lax.broadcasted_iota(I32, (N_KEYS, LANES), 0)

    def list_body(l, carry):
        st = lax.dot_general(sk_ref[l], q_scr[l], _NT, preferred_element_type=F32)
        for g in range(groups):
            cs = slice(g * LANES, (g + 1) * LANES)
            vals, idx = _extract_topk(st[:, cs], key_iota, PEER_TOPK)
            sv_scr[l, :, cs] = vals
            si_scr[l, :, cs] = idx
        return carry

    lax.fori_loop(0, n_lists, list_body, 0)

    j_iota = lax.broadcasted_iota(I32, (PEER_TOPK, LANES), 0)

    def head_body(hh, carry):
        for g in range(groups):
            cs = slice(g * LANES, (g + 1) * LANES)
            a = sv_scr[2 * hh, :, cs]
            b = sv_scr[2 * hh + 1, :, cs]
            ia = si_scr[2 * hh, :, cs]
            ib = si_scr[2 * hh + 1, :, cs]
            kb = (j_iota << KEY_SHIFT) + ib
            cand = jnp.concatenate([a[i:i + 1] + b for i in range(PEER_TOPK)], axis=0)
            key = jnp.concatenate(
                [kb + (ia[i:i + 1] * N_KEYS + ((i * PEER_TOPK) << KEY_SHIFT)) for i in range(PEER_TOPK)],
                axis=0)
            top_s, top_key = _extract_topk(cand, key, PEER_TOPK)
            e = jnp.exp(top_s - top_s[0:1])
            eid_ref[hh, :, cs] = top_key & ((1 << KEY_SHIFT) - 1)
            gate_ref[hh, :, cs] = e / jnp.sum(e, axis=0, keepdims=True)
        return carry

    lax.fori_loop(0, PEER_HEADS, head_body, 0)


def _route(h, g, wq_bf, sk_bf):
    n, d = h.shape
    tm = TOK_TILE
    nt = n // tm
    n_lists = 2 * PEER_HEADS
    out_blk = pl.BlockSpec((None, PEER_HEADS, PEER_TOPK, tm), lambda i: (i, 0, 0, 0))
    return pl.pallas_call(
        functools.partial(_route_body, tm=tm),
        grid=(nt,),
        in_specs=[pl.BlockSpec((tm, d), lambda i: (i, 0)), _resident((1, d)),
                  _resident(wq_bf.shape), _resident(sk_bf.shape)],
        out_specs=[pl.BlockSpec((tm, d), lambda i: (i, 0)), out_blk, out_blk],
        out_shape=[jax.ShapeDtypeStruct((n, d), F32),
                   jax.ShapeDtypeStruct((nt, PEER_HEADS, PEER_TOPK, tm), I32),
                   jax.ShapeDtypeStruct((nt, PEER_HEADS, PEER_TOPK, tm), F32)],
        scratch_shapes=[pltpu.VMEM((n_lists, tm, PEER_HALF), BF16),
                        pltpu.VMEM((n_lists, PEER_TOPK, tm), F32),
                        pltpu.VMEM((n_lists, PEER_TOPK, tm), I32)],
        compiler_params=_params("parallel"),
    )(h, g, wq_bf, sk_bf)


def _pack_table(tab):
    n, d = tab.shape
    bits = lax.bitcast_convert_type(tab.astype(BF16), jnp.uint16).astype(jnp.uint32)
    words = (bits[:, d // 2:] << 16) | bits[:, :d // 2]
    return lax.bitcast_convert_type(words, I32).reshape(n * ROW_WORDS, LANES)


def _unpack(words):
    lo = lax.bitcast_convert_type(words << 16, F32)
    hi = lax.bitcast_convert_type(words & jnp.int32(-65536), F32)
    return lo, hi


def _gather_row(tab_ref, idx):
    return tab_ref[pl.ds(pl.multiple_of(idx * ROW_WORDS, ROW_WORDS), ROW_WORDS), :]


def _upass_body(eid_ref, xn_ref, gate_ref, tab_ref, w_ref, gbuf, pbuf, abuf, *, tb):
    sub = lax.broadcasted_iota(I32, (SUBLANES, LANES), 0)
    lane = lax.broadcasted_iota(I32, (N_SLOTS, LANES), 1)
    abuf[...] = jnp.zeros(abuf.shape, F32)
    pairs = N_SLOTS * ROW_WORDS // SUBLANES

    def tok(t, carry):
        for r in range(N_SLOTS):
            gbuf[r * ROW_WORDS:(r + 1) * ROW_WORDS, :] = _gather_row(tab_ref, eid_ref[r, t])
        x = xn_ref[t]
        xr = pltpu.roll(x, ROW_WORDS, 0)
        t_lo = jnp.where(sub < ROW_WORDS, x, xr)
        t_hi = jnp.where(sub < ROW_WORDS, xr, x)
        lo, hi = _unpack(gbuf[...].reshape(pairs, SUBLANES, LANES))
        prod = lo * t_lo[None] + hi * t_hi[None]
        pbuf[...] = prod.reshape(N_SLOTS * ROW_WORDS, LANES)
        y = pbuf[pl.ds(0, N_SLOTS, stride=ROW_WORDS), :]
        for s in range(1, ROW_WORDS):
            y = y + pbuf[pl.ds(s, N_SLOTS, stride=ROW_WORDS), :]
        col = jnp.sum(y, axis=-1, keepdims=True)
        abuf[...] = jnp.where(lane == t, col, abuf[...])
        return carry

    lax.fori_loop(0, tb, tok, 0)
    a = abuf[...]
    act = 0.5 * a * (1.0 + lax.erf(a * math.sqrt(0.5)))
    w_ref[...] = gate_ref[...] * act


def _table_spec(tab):
    return pl.BlockSpec(tab.shape, lambda i: (0, 0), pipeline_mode=pl.Buffered(1))


def _smem_block():
    return pl.BlockSpec((None, N_SLOTS, GATHER_TILE), lambda i: (i, 0, 0), memory_space=pltpu.SMEM)


def _upass(eid, xn3, gate, tab):
    nt = eid.shape[0]
    tb = GATHER_TILE
    return pl.pallas_call(
        functools.partial(_upass_body, tb=tb),
        grid=(nt,),
        in_specs=[_smem_block(),
                  pl.BlockSpec((tb, SUBLANES, LANES), lambda i: (i, 0, 0)),
                  pl.BlockSpec((None, N_SLOTS, tb), lambda i: (i, 0, 0)),
                  _table_spec(tab)],
        out_specs=pl.BlockSpec((None, N_SLOTS, tb), lambda i: (i, 0, 0)),
        out_shape=jax.ShapeDtypeStruct((nt, N_SLOTS, tb), F32),
        scratch_shapes=[pltpu.VMEM((N_SLOTS * ROW_WORDS, LANES), I32),
                        pltpu.VMEM((N_SLOTS * ROW_WORDS, LANES), F32),
                        pltpu.VMEM((N_SLOTS, LANES), F32)],
        compiler_params=_params("parallel"),
    )(eid, xn3, gate, tab)


def _vpass_body(eid_ref, w_ref, h_ref, g_ref, tab_ref, y_ref, pbuf, *, tb):
    n_acc = 4

    def tok(t, carry):
        lo = [None] * n_acc
        hi = [None] * n_acc
        for r in range(N_SLOTS):
            w = w_ref[r, t]
            rlo, rhi = _unpack(_gather_row(tab_ref, eid_ref[r, t]))
            a = r % n_acc
            lo[a] = w * rlo if lo[a] is None else lo[a] + w * rlo
            hi[a] = w * rhi if hi[a] is None else hi[a] + w * rhi
        pbuf[t] = jnp.concatenate([(lo[0] + lo[1]) + (lo[2] + lo[3]),
                                   (hi[0] + hi[1]) + (hi[2] + hi[3])], axis=0)
        return carry

    lax.fori_loop(0, tb, tok, 0)
    hh = h_ref[...] + pbuf[...]
    ss = jnp.sum(jnp.sum(hh * hh, axis=2, keepdims=True), axis=1, keepdims=True)
    d = SUBLANES * LANES
    y_ref[...] = (hh * lax.rsqrt(ss / d + EPS)) * g_ref[...][None]


def _vpass(eid, w, h3, gfin, tab):
    nt = eid.shape[0]
    tb = GATHER_TILE
    tok_blk = pl.BlockSpec((tb, SUBLANES, LANES), lambda i: (i, 0, 0))
    return pl.pallas_call(
        functools.partial(_vpass_body, tb=tb),
        grid=(nt,),
        in_specs=[_smem_block(), _smem_block(), tok_blk, _resident((SUBLANES, LANES)), _table_spec(tab)],
        out_specs=tok_blk,
        out_shape=jax.ShapeDtypeStruct(h3.shape, F32),
        scratch_shapes=[pltpu.VMEM((tb, SUBLANES, LANES), F32)],
        compiler_params=_params("parallel"),
    )(eid, w, h3, gfin, tab)


def _peer_and_final_norm(h, g_ffn, wq_bf, sk_bf, u_tab, v_tab, g_final):
    n, d = h.shape
    xn, eid, gate = _route(h, g_ffn, wq_bf, sk_bf)
    nt = n // GATHER_TILE
    split = TOK_TILE // GATHER_TILE

    def regroup(a):
        a = a.reshape(n // TOK_TILE, N_SLOTS, split, GATHER_TILE)
        return a.transpose(0, 2, 1, 3).reshape(nt, N_SLOTS, GATHER_TILE)

    eid, gate = regroup(eid), regroup(gate)
    w = _upass(eid, xn.reshape(n, SUBLANES, LANES), gate, u_tab)
    y = _vpass(eid, w, h.reshape(n, SUBLANES, LANES), g_final.reshape(SUBLANES, LANES), v_tab)
    return y.reshape(n, d)


def kernel(x_prompt, x_sample, cache_k, cache_v, state_pool, page_table, norm_mix_g, w_in, pool_group_w,
           pool_scale, lambda_q1, lambda_k1, lambda_q2, lambda_k2, subln_g, rel_bias, w_branch_pool,
           w_branch_attn, w_out, norm_ffn_g, peer_wq, peer_subkeys, peer_u, peer_v, norm_final_g):
    depth = w_in.shape[0]
    assert depth == 1
    b_p, s_p, d = x_prompt.shape
    b_s, t_s, _ = x_sample.shape
    psz = cache_k.shape[2]
    past = page_table.shape[1] * psz
    width = N_HEADS * V_DIM
    assert ATT_BLOCK >= MAX_DISTANCE and s_p % ATT_BLOCK == 0
    l = 0
    lam_init = 0.8 - 0.6 * math.exp(-0.3 * l)

    w_in_bf = w_in[l].astype(BF16)
    gw_bf = pool_group_w[l].astype(BF16)
    wbp, wba, wout = (w[l].astype(BF16) for w in (w_branch_pool, w_branch_attn, w_out))
    wq_bf = peer_wq[l].astype(BF16)
    sk_bf = peer_subkeys[l].astype(BF16).reshape(2 * PEER_HEADS, N_KEYS, PEER_HALF)
    u_tab = _pack_table(peer_u[l])
    v_tab = _pack_table(peer_v[l])
    g_mix = norm_mix_g[l].reshape(1, d)
    g_ffn = norm_ffn_g[l].reshape(1, d)
    scale = pool_scale[l].reshape(1, -1)
    sg = subln_g[l].reshape(1, V_DIM)
    lams = tuple(a[l].astype(F32).reshape(1, HEAD_DIM) for a in (lambda_q1, lambda_k1, lambda_q2, lambda_k2))

    r = jnp.arange(ATT_BLOCK, dtype=I32)
    dist = jnp.stack([dj * ATT_BLOCK + r[:, None] - r[None, :] for dj in range(3)])
    bias_tiles = _rel_bias(rel_bias, dist)

    tq = jnp.arange(t_s, dtype=I32)
    dist_past = past + tq[:, None] - jnp.arange(past, dtype=I32)[None, :]
    bias_past = _rel_bias(rel_bias, dist_past)
    rows = 2 * N_HEADS * t_s
    step_keys = PAGES_PER_STEP * psz
    bias_past = jnp.tile(bias_past.reshape(N_HEADS * t_s, past), (2, 1))
    bias_past = bias_past.reshape(rows, past // step_keys, step_keys).transpose(1, 0, 2)
    dist_new = tq[:, None] - jnp.arange(psz, dtype=I32)[None, :]
    dist_new = jnp.where(jnp.arange(psz)[None, :] < t_s, dist_new, -1)
    bias_new = jnp.tile(_rel_bias(rel_bias, dist_new).reshape(N_HEADS * t_s, psz), (2, 1))

    def token_stages(x, hist, pos0, attention):
        b, length, _ = x.shape
        n = b * length
        u, q, k, v, gp, ga = _inproj(x.reshape(n, d), g_mix, w_in_bf)
        pool_o = _pool(hist, u.reshape(b, length, -1), gw_bf, scale, pos0)
        attn_o = attention(q, k, v)
        h = _merge(x.reshape(n, d), pool_o.reshape(n, -1), attn_o, gp, ga, wbp, wba, wout)
        y = _peer_and_final_norm(h, g_ffn, wq_bf, sk_bf, u_tab, v_tab, norm_final_g)
        return y.reshape(b, length, d), u, k, v

    hist_p = jnp.zeros((b_p, HIST_ROWS, width), F32)
    attn_p = lambda q, k, v: _prompt_attention(q, k, v, bias_tiles, lams, sg, b_p, s_p, lam_init)
    y_p, u_p, k_p, v_p = token_stages(x_prompt, hist_p, 0, attn_p)

    hist_s = jnp.pad(state_pool[l], ((0, 0), (HIST_ROWS - POOL_STATE, 0), (0, 0)))
    ck = cache_k[l].reshape(-1, psz, width)
    cv = cache_v[l].reshape(-1, psz, width)

    def attn_s(q, k, v):
        q3 = q.reshape(b_s, t_s, width)
        colmap = jnp.arange(width, dtype=I32) // HEAD_DIM
        rowmap = (jnp.arange(rows, dtype=I32) // t_s)
        rowmap = (rowmap % N_HEADS) * 2 + rowmap // N_HEADS
        sel = (rowmap[:, None] == colmap[None, :])
        qbd = jnp.where(sel[None], jnp.tile(q3, (1, 2 * N_HEADS, 1)), jnp.zeros((), BF16))
        pad = ((0, 0), (0, psz - t_s), (0, 0))
        k_new = jnp.pad(k.reshape(b_s, t_s, width), pad)
        v_new = jnp.pad(v.reshape(b_s, t_s, width), pad)
        o = _sample_attention(page_table, qbd, bias_past, bias_new, k_new, v_new, ck, cv, lams, sg,
                              t_s, lam_init)
        return o.reshape(b_s * t_s, width)

    y_s, u_s, k_s, v_s = token_stages(x_sample, hist_s, past, attn_s)

    kv_shape = lambda b, length: (1, b, length, N_HEADS, V_DIM)
    u_p3 = u_p.reshape(b_p, s_p, width)
    u_s3 = u_s.reshape(b_s, t_s, width)
    pool_prompt = u_p3[:, -POOL_STATE:][None]
    pool_sample = jnp.concatenate([state_pool[l], u_s3], axis=1)[:, -POOL_STATE:][None]
    return (y_p, y_s, k_p.reshape(kv_shape(b_p, s_p)), v_p.reshape(kv_shape(b_p, s_p)), pool_prompt,
            k_s.reshape(kv_shape(b_s, t_s)), v_s.reshape(kv_shape(b_s, t_s)), pool_sample)
```

```python
import functools
import math

import jax
import jax.numpy as jnp
from jax import lax
from jax.experimental import pallas as pl
from jax.experimental.pallas import tpu as pltpu

F32 = jnp.float32
BF16 = jnp.bfloat16
I32 = jnp.int32

EPS = 1e-6
NEG_INF = -1e30

LANES = 128
SUBLANES = 8
VMEM_LIMIT_BYTES = 56 * 1024 * 1024

POOL_WINDOWS = (2, 4, 8, 16)
POOL_STATE = max(POOL_WINDOWS) - 1
HIST_ROWS = 16
N_HEADS = 4
HEAD_DIM = 64
V_DIM = 2 * HEAD_DIM
N_BUCKETS = 32
MAX_DISTANCE = 128
PEER_HEADS = 8
PEER_TOPK = 16
N_KEYS = 128
PEER_HALF = 128
N_SLOTS = PEER_HEADS * PEER_TOPK
ROW_WORDS = 4
KEY_SHIFT = 14

TOK_TILE = 256
ATT_BLOCK = 256
PAGES_PER_STEP = 8
GATHER_TILE = 128

_NT = (((1,), (1,)), ((), ()))


def _params(*sem):
    return pltpu.CompilerParams(dimension_semantics=sem, vmem_limit_bytes=VMEM_LIMIT_BYTES)


def _rms(x, g):
    return (x * lax.rsqrt(jnp.mean(x * x, axis=-1, keepdims=True) + EPS)) * g


def _resident(shape):
    nd = len(shape)
    return pl.BlockSpec(shape, lambda *_: (0,) * nd)


def _inproj_body(x_ref, g_ref, w_ref, u_ref, q_ref, k_ref, v_ref, gp_ref, ga_ref, *, splits):
    xb = _rms(x_ref[...], g_ref[...]).astype(BF16)

    def proj(i):
        return jnp.dot(xb, w_ref[:, splits[i]:splits[i + 1]], preferred_element_type=F32)

    u_ref[...] = proj(0)
    q_ref[...] = (proj(1) * (HEAD_DIM ** -0.5)).astype(BF16)
    k_ref[...] = proj(2)
    v_ref[...] = proj(3)
    gp_ref[...] = jax.nn.sigmoid(proj(4))
    ga_ref[...] = jax.nn.sigmoid(proj(5))


def _inproj(x, g, w_in_bf):
    n, d = x.shape
    pw = 512
    splits = (0, pw, 2 * pw, 3 * pw, 4 * pw, 4 * pw + d, 4 * pw + 2 * d)
    tm = TOK_TILE
    row = lambda c: pl.BlockSpec((tm, c), lambda i: (i, 0))
    return pl.pallas_call(
        functools.partial(_inproj_body, splits=splits),
        grid=(n // tm,),
        in_specs=[row(d), _resident((1, d)), _resident(w_in_bf.shape)],
        out_specs=[row(pw), row(pw), row(pw), row(pw), row(d), row(d)],
        out_shape=[jax.ShapeDtypeStruct((n, pw), F32), jax.ShapeDtypeStruct((n, pw), BF16),
                   jax.ShapeDtypeStruct((n, pw), F32), jax.ShapeDtypeStruct((n, pw), F32),
                   jax.ShapeDtypeStruct((n, d), F32), jax.ShapeDtypeStruct((n, d), F32)],
        compiler_params=_params("parallel"),
    )(x, g, w_in_bf)


def _pool_body(hist_ref, u_ref, gw_ref, sc_ref, o_ref, buf, *, length, pos0):
    buf[0:HIST_ROWS, :] = hist_ref[...]
    buf[HIST_ROWS:HIST_ROWS + length, :] = u_ref[...]
    t = lax.broadcasted_iota(I32, (length, 1), 0)
    for gi, w in enumerate(POOL_WINDOWS):
        c0 = gi * LANES
        ug = buf[HIST_ROWS:HIST_ROWS + length, c0:c0 + LANES]
        acc = ug
        for j in range(1, w):
            acc = acc + buf[HIST_ROWS - j:HIST_ROWS - j + length, c0:c0 + LANES]
        cnt = jnp.minimum(t + (pos0 + 1), w).astype(F32)
        mixed = acc / cnt - ug
        y = jnp.dot(mixed.astype(BF16), gw_ref[gi], preferred_element_type=F32)
        o_ref[:, c0:c0 + LANES] = y * sc_ref[:, c0:c0 + LANES]


def _pool(hist, u, gw_bf, scale, pos0):
    b, length, pw = u.shape
    return pl.pallas_call(
        functools.partial(_pool_body, length=length, pos0=pos0),
        grid=(b,),
        in_specs=[pl.BlockSpec((None, HIST_ROWS, pw), lambda i: (i, 0, 0)),
                  pl.BlockSpec((None, length, pw), lambda i: (i, 0, 0)),
                  _resident(gw_bf.shape), _resident((1, pw))],
        out_specs=pl.BlockSpec((None, length, pw), lambda i: (i, 0, 0)),
        out_shape=jax.ShapeDtypeStruct((b, length, pw), F32),
        scratch_shapes=[pltpu.VMEM((HIST_ROWS + length, pw), F32)],
        compiler_params=_params("parallel"),
    )(hist, u, gw_bf, scale)


def _lambda(lq1, lk1, lq2, lk2, lam_init):
    d1 = jnp.sum(lq1[...] * lk1[...], axis=-1, keepdims=True)
    d2 = jnp.sum(lq2[...] * lk2[...], axis=-1, keepdims=True)
    return jnp.exp(d1) - jnp.exp(d2) + lam_init


def _softmax_update(s, vals, m, l, acc):
    mn = jnp.maximum(m, jnp.max(s, axis=-1, keepdims=True))
    alpha = jnp.exp(m - mn)
    p = jnp.exp(s - mn)
    l = alpha * l + jnp.sum(p, axis=-1, keepdims=True)
    pb = p.astype(BF16)
    pv = None
    for (c0, c1), vb in vals:
        term = jnp.dot(pb[:, c0:c1], vb, preferred_element_type=F32)
        pv = term if pv is None else pv + term
    return mn, l, alpha * acc + pv


def _head_norm(o, sg, lam_init):
    return _rms(o, sg) * (1.0 - lam_init)


def _t5_bucket(n):
    max_exact = N_BUCKETS // 2
    nf = jnp.maximum(n, 1).astype(F32)
    large = max_exact + (jnp.log(nf / max_exact) / math.log(MAX_DISTANCE / max_exact)
                         * (N_BUCKETS - max_exact)).astype(I32)
    large = jnp.minimum(large, N_BUCKETS - 1)
    return jnp.where(n < max_exact, n, large)


def _rel_bias(rel_bias, n):
    bucket = _t5_bucket(jnp.maximum(n, 0))
    table = rel_bias.astype(F32)
    b = jnp.zeros((table.shape[1],) + n.shape, F32)
    for i in range(N_BUCKETS):
        b = jnp.where(bucket == i, table[i].reshape((-1,) + (1,) * n.ndim), b)
    return jnp.where(n >= 0, b, NEG_INF)


def _pattn_body(q_ref, k_ref, v_ref, bias_ref, lq1, lk1, lq2, lk2, sg_ref, o_ref, kb, vb,
                *, blk, lam_init):
    i = pl.program_id(2)

    @pl.when(i == 0)
    def _():
        kb[...] = k_ref[...].astype(BF16)
        vb[...] = v_ref[...].astype(BF16)

    q = q_ref[...]
    lane = lax.broadcasted_iota(I32, q.shape, 1)
    zero = jnp.zeros_like(q)
    q1 = jnp.where(lane < HEAD_DIM, q, zero)
    q2 = jnp.where(lane < HEAD_DIM, zero, q)

    def step(j, carry):
        m1, l1, a1, m2, l2, a2 = carry
        off = pl.multiple_of(j * blk, blk)
        kc = kb[pl.ds(off, blk), :]
        vc = vb[pl.ds(off, blk), :]
        bt = bias_ref[jnp.minimum(i - j, 2)]
        s1 = lax.dot_general(q1, kc, _NT, preferred_element_type=F32) + bt
        s2 = lax.dot_general(q2, kc, _NT, preferred_element_type=F32) + bt
        vals = [((0, blk), vc)]
        m1, l1, a1 = _softmax_update(s1, vals, m1, l1, a1)
        m2, l2, a2 = _softmax_update(s2, vals, m2, l2, a2)
        return m1, l1, a1, m2, l2, a2

    col = lambda v: jnp.full((blk, 1), v, F32)
    acc0 = jnp.zeros((blk, V_DIM), F32)
    init = (col(-jnp.inf), col(0.0), acc0, col(-jnp.inf), col(0.0), acc0)
    _, l1, a1, _, l2, a2 = lax.fori_loop(0, i + 1, step, init)
    lam = _lambda(lq1, lk1, lq2, lk2, lam_init)
    o = a1 / l1 - lam * (a2 / l2)
    o_ref[...] = _head_norm(o, sg_ref[...], lam_init)


def _prompt_attention(q, k, v, bias_tiles, lams, sg, batch, seq, lam_init):
    blk = ATT_BLOCK
    nq = seq // blk
    hw = V_DIM
    lam_spec = _resident((1, HEAD_DIM))
    return pl.pallas_call(
        functools.partial(_pattn_body, blk=blk, lam_init=lam_init),
        grid=(batch, N_HEADS, nq),
        in_specs=[pl.BlockSpec((blk, hw), lambda b, h, i: (b * nq + i, h)),
                  pl.BlockSpec((seq, hw), lambda b, h, i: (b, h)),
                  pl.BlockSpec((seq, hw), lambda b, h, i: (b, h)),
                  pl.BlockSpec((None, 3, blk, blk), lambda b, h, i: (h, 0, 0, 0)),
                  lam_spec, lam_spec, lam_spec, lam_spec, _resident((1, hw))],
        out_specs=pl.BlockSpec((blk, hw), lambda b, h, i: (b * nq + i, h)),
        out_shape=jax.ShapeDtypeStruct((batch * seq, N_HEADS * hw), F32),
        scratch_shapes=[pltpu.VMEM((seq, hw), BF16), pltpu.VMEM((seq, hw), BF16)],
        compiler_params=_params("parallel", "parallel", "arbitrary"),
    )(q, k, v, bias_tiles, *lams, sg)


def _sattn_body(pt_ref, qbd_ref, bias_ref, bnew_ref, knew_ref, vnew_ref, lq1, lk1, lq2, lk2, sg_ref,
                *rest, pages, t_new, lam_init):
    k_refs = rest[:pages]
    v_refs = rest[pages:2 * pages]
    o_ref, m_s, l_s, acc_s = rest[2 * pages:]
    j = pl.program_id(1)
    psz = k_refs[0].shape[0]

    @pl.when(j == 0)
    def _():
        m_s[...] = jnp.full(m_s.shape, -jnp.inf, F32)
        l_s[...] = jnp.zeros(l_s.shape, F32)
        acc_s[...] = jnp.zeros(acc_s.shape, F32)

    qbd = qbd_ref[...]

    def update(k_blocks, v_blocks, bias):
        s = jnp.concatenate(
            [lax.dot_general(qbd, kb.astype(BF16), _NT, preferred_element_type=F32) for kb in k_blocks],
            axis=1) + bias
        vals = [((p * psz, (p + 1) * psz), vb.astype(BF16)) for p, vb in enumerate(v_blocks)]
        m, l, acc = _softmax_update(s, vals, m_s[...], l_s[...], acc_s[...])
        m_s[...] = m
        l_s[...] = l
        acc_s[...] = acc

    update([r[...] for r in k_refs], [r[...] for r in v_refs], bias_ref[j])

    @pl.when(j == pl.num_programs(1) - 1)
    def _():
        update([knew_ref[...]], [vnew_ref[...]], bnew_ref[...])
        lam = _lambda(lq1, lk1, lq2, lk2, lam_init)
        acc = acc_s[...]
        l = l_s[...]
        half = N_HEADS * t_new
        for h in range(N_HEADS):
            r0, c0 = h * t_new, h * V_DIM
            o1 = acc[r0:r0 + t_new, c0:c0 + V_DIM] / l[r0:r0 + t_new]
            o2 = acc[half + r0:half + r0 + t_new, c0:c0 + V_DIM] / l[half + r0:half + r0 + t_new]
            o_ref[:, c0:c0 + V_DIM] = _head_norm(o1 - lam * o2, sg_ref[...], lam_init)


def _sample_attention(page_table, qbd, bias_past, bias_new, k_new, v_new, cache_k, cache_v, lams, sg,
                      t_new, lam_init):
    b, n_pages = page_table.shape
    pages = PAGES_PER_STEP
    nj = n_pages // pages
    psz, width = cache_k.shape[1:]
    rows = qbd.shape[1]
    lam_spec = _resident((1, HEAD_DIM))

    def page_spec(p):
        return pl.BlockSpec((None, psz, width), lambda i, j, pt: (pt[i, j * pages + p], 0, 0))

    per_batch = lambda r: pl.BlockSpec((None, r, width), lambda i, j, pt: (i, 0, 0))
    grid_spec = pltpu.PrefetchScalarGridSpec(
        num_scalar_prefetch=1,
        grid=(b, nj),
        in_specs=[per_batch(rows), _resident(bias_past.shape), _resident(bias_new.shape),
                  per_batch(psz), per_batch(psz),
                  lam_spec, lam_spec, lam_spec, lam_spec, _resident((1, V_DIM))]
                 + [page_spec(p) for p in range(pages)] + [page_spec(p) for p in range(pages)],
        out_specs=per_batch(t_new),
        scratch_shapes=[pltpu.VMEM((rows, 1), F32), pltpu.VMEM((rows, 1), F32),
                        pltpu.VMEM((rows, width), F32)],
    )
    return pl.pallas_call(
        functools.partial(_sattn_body, pages=pages, t_new=t_new, lam_init=lam_init),
        grid_spec=grid_spec,
        out_shape=jax.ShapeDtypeStruct((b, t_new, width), F32),
        compiler_params=_params("parallel", "arbitrary"),
    )(page_table, qbd, bias_past, bias_new, k_new, v_new, *lams, sg,
      *([cache_k] * pages), *([cache_v] * pages))


def _merge_body(x_ref, po_ref, ao_ref, gp_ref, ga_ref, wbp_ref, wba_ref, wout_ref, h_ref):
    a = jnp.dot(po_ref[...].astype(BF16), wbp_ref[...], preferred_element_type=F32)
    b = jnp.dot(ao_ref[...].astype(BF16), wba_ref[...], preferred_element_type=F32)
    m = gp_ref[...] * a + ga_ref[...] * b
    h_ref[...] = x_ref[...] + jnp.dot(m.astype(BF16), wout_ref[...], preferred_element_type=F32)


def _merge(x, pool_o, attn_o, gp, ga, wbp, wba, wout):
    n, d = x.shape
    tm = TOK_TILE
    row = lambda c: pl.BlockSpec((tm, c), lambda i: (i, 0))
    return pl.pallas_call(
        _merge_body,
        grid=(n // tm,),
        in_specs=[row(d), row(pool_o.shape[1]), row(attn_o.shape[1]), row(d), row(d),
                  _resident(wbp.shape), _resident(wba.shape), _resident(wout.shape)],
        out_specs=row(d),
        out_shape=jax.ShapeDtypeStruct((n, d), F32),
        compiler_params=_params("parallel"),
    )(x, pool_o, attn_o, gp, ga, wbp, wba, wout)


def _extract_topk(x, key, k):
    big = jnp.iinfo(jnp.int32).max
    vals, keys = [], []
    for _ in range(k):
        m = jnp.max(x, axis=0, keepdims=True)
        kmin = jnp.min(jnp.where(x == m, key, big), axis=0, keepdims=True)
        x = jnp.where(key == kmin, -jnp.inf, x)
        vals.append(m)
        keys.append(kmin)
    return jnp.concatenate(vals, axis=0), jnp.concatenate(keys, axis=0)


def _route_body(h_ref, g_ref, wq_ref, sk_ref, xn_ref, eid_ref, gate_ref, q_scr, sv_scr, si_scr,
                eid_scr, gate_scr, *, tm):
    xn = _rms(h_ref[...], g_ref[...])
    xn_ref[...] = xn
    q = jnp.dot(xn.astype(BF16), wq_ref[...], preferred_element_type=F32)
    n_lists = 2 * PEER_HEADS
    for l in range(n_lists):
        q_scr[l] = q[:, l * PEER_HALF:(l + 1) * PEER_HALF].astype(BF16)
    groups = tm // LANES
    key_iota = lax.broadcasted_iota(I32, (N_KEYS, LANES), 0)

    def list_body(l, carry):
        st = lax.dot_general(sk_ref[l], q_scr[l], _NT, preferred_element_type=F32)
        for g in range(groups):
            cs = slice(g * LANES, (g + 1) * LANES)
            vals, idx = _extract_topk(st[:, cs], key_iota, PEER_TOPK)
            sv_scr[l, :, cs] = vals
            si_scr[l, :, cs] = idx
        return carry

    lax.fori_loop(0, n_lists, list_body, 0)

    j_iota = lax.broadcasted_iota(I32, (PEER_TOPK, LANES), 0)

    def head_body(hh, carry):
        rows = pl.ds(pl.multiple_of(hh * PEER_TOPK, PEER_TOPK), PEER_TOPK)
        for g in range(groups):
            cs = slice(g * LANES, (g + 1) * LANES)
            a = sv_scr[2 * hh, :, cs]
            b = sv_scr[2 * hh + 1, :, cs]
            ia = si_scr[2 * hh, :, cs]
            ib = si_scr[2 * hh + 1, :, cs]
            kb = (j_iota << KEY_SHIFT) + ib
            row_key = lambda i: ia[i:i + 1] * N_KEYS + ((i * PEER_TOPK) << KEY_SHIFT)
            half = PEER_TOPK // 2
            cands = [a[0:1] + b]
            keys = [kb + row_key(0)]
            for i in range(1, half):
                valid = j_iota[:half] < PEER_TOPK // (i + 1)
                cands.append(jnp.where(valid, a[i:i + 1] + b[:half], -jnp.inf))
                keys.append(kb[:half] + row_key(i))
            cands.append(a[half:] + b[0:1])
            keys.append(((j_iota[half:] * PEER_TOPK) << KEY_SHIFT) + ia[half:] * N_KEYS + ib[0:1])
            top_s, top_key = _extract_topk(jnp.concatenate(cands, axis=0), jnp.concatenate(keys, axis=0),
                                           PEER_TOPK)
            e = jnp.exp(top_s - top_s[0:1])
            eid_scr[rows, cs] = (top_key & ((1 << KEY_SHIFT) - 1)) * ROW_WORDS
            gate_scr[rows, cs] = e / jnp.sum(e, axis=0, keepdims=True)
        return carry

    lax.fori_loop(0, PEER_HEADS, head_body, 0)
    eid_ref[...] = eid_scr[...].T
    gate_ref[...] = gate_scr[...].T


def _route(h, g, wq_bf, sk_bf):
    n, d = h.shape
    tm = TOK_TILE
    n_lists = 2 * PEER_HEADS
    row = lambda c: pl.BlockSpec((tm, c), lambda i: (i, 0))
    return pl.pallas_call(
        functools.partial(_route_body, tm=tm),
        grid=(n // tm,),
        in_specs=[row(d), _resident((1, d)), _resident(wq_bf.shape), _resident(sk_bf.shape)],
        out_specs=[row(d), row(N_SLOTS), row(N_SLOTS)],
        out_shape=[jax.ShapeDtypeStruct((n, d), F32), jax.ShapeDtypeStruct((n, N_SLOTS), I32),
                   jax.ShapeDtypeStruct((n, N_SLOTS), F32)],
        scratch_shapes=[pltpu.VMEM((n_lists, tm, PEER_HALF), BF16),
                        pltpu.VMEM((n_lists, PEER_TOPK, tm), F32),
                        pltpu.VMEM((n_lists, PEER_TOPK, tm), I32),
                        pltpu.VMEM((N_SLOTS, tm), I32),
                        pltpu.VMEM((N_SLOTS, tm), F32)],
        compiler_params=_params("parallel"),
    )(h, g, wq_bf, sk_bf)


CHUNKS = 2 * ROW_WORDS
GB_ROWS = N_SLOTS * CHUNKS
N_GBUF = 4
_GBUFS = [pltpu.VMEM((N_SLOTS * ROW_WORDS, LANES), I32)] * N_GBUF


def _pack_table(tab):
    n, d = tab.shape
    bits = lax.bitcast_convert_type(tab.astype(BF16), jnp.uint16).astype(jnp.uint32)
    bits = bits.reshape(n, ROW_WORDS, 2, LANES)
    words = (bits[:, :, 1, :] << 16) | bits[:, :, 0, :]
    return lax.bitcast_convert_type(words, I32).reshape(n * ROW_WORDS, LANES)


def _stage_rows(eid_ref, t, tab_ref, gbuf):
    for r in range(N_SLOTS):
        off = pl.multiple_of(eid_ref[t, r], ROW_WORDS)
        gbuf[r * ROW_WORDS:(r + 1) * ROW_WORDS, :] = tab_ref[pl.ds(off, ROW_WORDS), :]


def _staged(gbuf):
    return pltpu.bitcast(gbuf[...], BF16)


def _for_each_token(tb, eid_ref, tab_ref, gbufs, start, finish):
    _stage_rows(eid_ref, 0, tab_ref, gbufs[0])
    _stage_rows(eid_ref, 1, tab_ref, gbufs[1])

    def pair(i, carry):
        t0 = 2 * i

        def run(cur, nxt):
            ya = start(t0, _staged(cur[0]))
            yb = start(t0 + 1, _staged(cur[1]))
            _stage_rows(eid_ref, jnp.minimum(t0 + 2, tb - 1), tab_ref, nxt[0])
            _stage_rows(eid_ref, jnp.minimum(t0 + 3, tb - 1), tab_ref, nxt[1])
            finish(t0, ya)
            finish(t0 + 1, yb)

        odd = i & 1
        pl.when(odd == 0)(lambda: run(gbufs[:2], gbufs[2:]))
        pl.when(odd == 1)(lambda: run(gbufs[2:], gbufs[:2]))
        return carry

    lax.fori_loop(0, tb // 2, pair, 0)


def _chunk_mask():
    k = lax.broadcasted_iota(I32, (CHUNKS, GB_ROWS), 1)
    q = lax.broadcasted_iota(I32, (CHUNKS, GB_ROWS), 0)
    return (k & (CHUNKS - 1)) == q


def _token_rows(t):
    return pl.ds(pl.multiple_of(t * CHUNKS, CHUNKS), CHUNKS)


def _upass_body(eid_ref, xn_ref, gate_ref, tab_ref, rexp_ref, rsum_ref, w_ref, *scratch, tb):
    gbufs, (xbuf, zbuf) = scratch[:N_GBUF], scratch[N_GBUF:]
    for q in range(CHUNKS):
        xbuf[pl.ds(q, tb, stride=CHUNKS), :] = xn_ref[:, q * LANES:(q + 1) * LANES]
    mask = _chunk_mask()

    def start(t, g):
        x8 = xbuf[_token_rows(t), :].astype(BF16)
        return lax.dot_general(x8, g, _NT, preferred_element_type=F32)

    def finish(t, y):
        zbuf[pl.ds(t, 1), :] = jnp.sum(jnp.where(mask, y, 0.0), axis=0, keepdims=True)

    _for_each_token(tb, eid_ref, tab_ref, gbufs, start, finish)
    a = jnp.dot(zbuf[...], rsum_ref[...], preferred_element_type=F32, precision=lax.Precision.HIGHEST)
    act = 0.5 * a * (1.0 + lax.erf(a * math.sqrt(0.5)))
    w = (gate_ref[...] * act).astype(BF16)
    w_ref[...] = jnp.dot(w, rexp_ref[...], preferred_element_type=F32)


def _table_spec(tab):
    return pl.BlockSpec(tab.shape, lambda i: (0, 0), pipeline_mode=pl.Buffered(1))


def _gather_specs(d):
    tb = GATHER_TILE
    smem = pl.BlockSpec((tb, N_SLOTS), lambda i: (i, 0), memory_space=pltpu.SMEM)
    return tb, smem, (lambda c: pl.BlockSpec((tb, c), lambda i: (i, 0)))


def _upass(eid, xn, gate, tab, rexp, rsum):
    n, d = xn.shape
    tb, smem, row = _gather_specs(d)
    return pl.pallas_call(
        functools.partial(_upass_body, tb=tb),
        grid=(n // tb,),
        in_specs=[smem, row(d), row(N_SLOTS), _table_spec(tab), _resident(rexp.shape), _resident(rsum.shape)],
        out_specs=row(GB_ROWS),
        out_shape=jax.ShapeDtypeStruct((n, GB_ROWS), F32),
        scratch_shapes=_GBUFS + [
                        pltpu.VMEM((tb * CHUNKS, LANES), F32),
                        pltpu.VMEM((tb, GB_ROWS), F32)],
        compiler_params=_params("parallel"),
    )(eid, xn, gate, tab, rexp, rsum)


def _vpass_body(eid_ref, w_ref, h_ref, g_ref, tab_ref, y_ref, *scratch, tb):
    gbufs, pbuf = scratch[:N_GBUF], scratch[N_GBUF]
    mask = _chunk_mask()

    def start(t, g):
        wsel = jnp.where(mask, w_ref[pl.ds(t, 1), :], 0.0).astype(BF16)
        return jnp.dot(wsel, g, preferred_element_type=F32)

    def finish(t, y):
        pbuf[_token_rows(t), :] = y

    _for_each_token(tb, eid_ref, tab_ref, gbufs, start, finish)
    peer = jnp.concatenate([pbuf[pl.ds(q, tb, stride=CHUNKS), :] for q in range(CHUNKS)], axis=1)
    y_ref[...] = _rms(h_ref[...] + peer, g_ref[...])


def _vpass(eid, w, h, gfin, tab):
    n, d = h.shape
    tb, smem, row = _gather_specs(d)
    return pl.pallas_call(
        functools.partial(_vpass_body, tb=tb),
        grid=(n // tb,),
        in_specs=[smem, row(GB_ROWS), row(d), _resident((1, d)), _table_spec(tab)],
        out_specs=row(d),
        out_shape=jax.ShapeDtypeStruct((n, d), F32),
        scratch_shapes=_GBUFS + [pltpu.VMEM((tb * CHUNKS, LANES), F32)],
        compiler_params=_params("parallel"),
    )(eid, w, h, gfin, tab)


def _peer_and_final_norm(h, g_ffn, wq_bf, sk_bf, u_tab, v_tab, g_final):
    xn, eid, gate = _route(h, g_ffn, wq_bf, sk_bf)
    slot_of_row = jnp.arange(GB_ROWS, dtype=I32) // CHUNKS
    rexp = (jnp.arange(N_SLOTS, dtype=I32)[:, None] == slot_of_row[None, :])
    w = _upass(eid, xn, gate, u_tab, rexp.astype(BF16), rexp.T.astype(F32))
    return _vpass(eid, w, h, g_final.reshape(1, -1), v_tab)


def kernel(x_prompt, x_sample, cache_k, cache_v, state_pool, page_table, norm_mix_g, w_in, pool_group_w,
           pool_scale, lambda_q1, lambda_k1, lambda_q2, lambda_k2, subln_g, rel_bias, w_branch_pool,
           w_branch_attn, w_out, norm_ffn_g, peer_wq, peer_subkeys, peer_u, peer_v, norm_final_g):
    depth = w_in.shape[0]
    assert depth == 1
    b_p, s_p, d = x_prompt.shape
    b_s, t_s, _ = x_sample.shape
    psz = cache_k.shape[2]
    past = page_table.shape[1] * psz
    width = N_HEADS * V_DIM
    assert ATT_BLOCK >= MAX_DISTANCE and s_p % ATT_BLOCK == 0
    l = 0
    lam_init = 0.8 - 0.6 * math.exp(-0.3 * l)

    w_in_bf = w_in[l].astype(BF16)
    gw_bf = pool_group_w[l].astype(BF16)
    wbp, wba, wout = (w[l].astype(BF16) for w in (w_branch_pool, w_branch_attn, w_out))
    wq_bf = peer_wq[l].astype(BF16)
    sk_bf = peer_subkeys[l].astype(BF16).reshape(2 * PEER_HEADS, N_KEYS, PEER_HALF)
    u_tab = _pack_table(peer_u.reshape(-1, d))
    v_tab = _pack_table(peer_v.reshape(-1, d))
    g_mix = norm_mix_g[l].reshape(1, d)
    g_ffn = norm_ffn_g[l].reshape(1, d)
    scale = pool_scale[l].reshape(1, -1)
    sg = subln_g[l].reshape(1, V_DIM)
    lams = tuple(a[l].astype(F32).reshape(1, HEAD_DIM) for a in (lambda_q1, lambda_k1, lambda_q2, lambda_k2))

    r = jnp.arange(ATT_BLOCK, dtype=I32)
    dist = jnp.stack([dj * ATT_BLOCK + r[:, None] - r[None, :] for dj in range(3)])
    bias_tiles = _rel_bias(rel_bias, dist)

    tq = jnp.arange(t_s, dtype=I32)
    dist_past = past + tq[:, None] - jnp.arange(past, dtype=I32)[None, :]
    bias_past = _rel_bias(rel_bias, dist_past)
    rows = 2 * N_HEADS * t_s
    step_keys = PAGES_PER_STEP * psz
    bias_past = jnp.tile(bias_past.reshape(N_HEADS * t_s, past), (2, 1))
    bias_past = bias_past.reshape(rows, past // step_keys, step_keys).transpose(1, 0, 2)
    dist_new = tq[:, None] - jnp.arange(psz, dtype=I32)[None, :]
    dist_new = jnp.where(jnp.arange(psz)[None, :] < t_s, dist_new, -1)
    bias_new = jnp.tile(_rel_bias(rel_bias, dist_new).reshape(N_HEADS * t_s, psz), (2, 1))

    def token_stages(x, hist, pos0, attention):
        b, length, _ = x.shape
        n = b * length
        u, q, k, v, gp, ga = _inproj(x.reshape(n, d), g_mix, w_in_bf)
        pool_o = _pool(hist, u.reshape(b, length, -1), gw_bf, scale, pos0)
        attn_o = attention(q, k, v)
        h = _merge(x.reshape(n, d), pool_o.reshape(n, -1), attn_o, gp, ga, wbp, wba, wout)
        y = _peer_and_final_norm(h, g_ffn, wq_bf, sk_bf, u_tab, v_tab, norm_final_g)
        return y.reshape(b, length, d), u, k, v

    hist_p = jnp.zeros((b_p, HIST_ROWS, width), F32)
    attn_p = lambda q, k, v: _prompt_attention(q, k, v, bias_tiles, lams, sg, b_p, s_p, lam_init)
    y_p, u_p, k_p, v_p = token_stages(x_prompt, hist_p, 0, attn_p)

    hist_s = jnp.pad(state_pool[l], ((0, 0), (HIST_ROWS - POOL_STATE, 0), (0, 0)))
    ck = cache_k.reshape(-1, psz, width)
    cv = cache_v.reshape(-1, psz, width)

    def attn_s(q, k, v):
        q3 = q.reshape(b_s, t_s, width)
        colmap = jnp.arange(width, dtype=I32) // HEAD_DIM
        rowmap = (jnp.arange(rows, dtype=I32) // t_s)
        rowmap = (rowmap % N_HEADS) * 2 + rowmap // N_HEADS
        sel = (rowmap[:, None] == colmap[None, :])
        qbd = jnp.where(sel[None], jnp.tile(q3, (1, 2 * N_HEADS, 1)), jnp.zeros((), BF16))
        pad = ((0, 0), (0, psz - t_s), (0, 0))
        k_new = jnp.pad(k.reshape(b_s, t_s, width), pad)
        v_new = jnp.pad(v.reshape(b_s, t_s, width), pad)
        o = _sample_attention(page_table, qbd, bias_past, bias_new, k_new, v_new, ck, cv, lams, sg,
                              t_s, lam_init)
        return o.reshape(b_s * t_s, width)

    y_s, u_s, k_s, v_s = token_stages(x_sample, hist_s, past, attn_s)

    kv_shape = lambda b, length: (1, b, length, N_HEADS, V_DIM)
    u_p3 = u_p.reshape(b_p, s_p, width)
    u_s3 = u_s.reshape(b_s, t_s, width)
    pool_prompt = u_p3[:, -POOL_STATE:][None]
    pool_sample = jnp.concatenate([state_pool[l], u_s3], axis=1)[:, -POOL_STATE:][None]
    return (y_p, y_s, k_p.reshape(kv_shape(b_p, s_p)), v_p.reshape(kv_shape(b_p, s_p)), pool_prompt,
            k_s.reshape(kv_shape(b_s, t_s)), v_s.reshape(kv_shape(b_s, t_s)), pool_sample)
```

```python
import functools
import math

import jax
import jax.numpy as jnp
from jax import lax
from jax.experimental import pallas as pl
from jax.experimental.pallas import tpu as pltpu

F32 = jnp.float32
BF16 = jnp.bfloat16
I32 = jnp.int32

EPS = 1e-6
NEG_INF = -1e30

LANES = 128
SUBLANES = 8
VMEM_LIMIT_BYTES = 56 * 1024 * 1024

POOL_WINDOWS = (2, 4, 8, 16)
POOL_STATE = max(POOL_WINDOWS) - 1
HIST_ROWS = 16
N_HEADS = 4
HEAD_DIM = 64
V_DIM = 2 * HEAD_DIM
N_BUCKETS = 32
MAX_DISTANCE = 128
PEER_HEADS = 8
PEER_TOPK = 16
N_KEYS = 128
PEER_HALF = 128
N_SLOTS = PEER_HEADS * PEER_TOPK
ROW_WORDS = 4
KEY_SHIFT = 14

TOK_TILE = 256
ATT_BLOCK = 256
PAGES_PER_STEP = 16
GATHER_TILE = 128

_NT = (((1,), (1,)), ((), ()))


def _params(*sem):
    return pltpu.CompilerParams(dimension_semantics=sem, vmem_limit_bytes=VMEM_LIMIT_BYTES)


def _rms(x, g):
    return (x * lax.rsqrt(jnp.mean(x * x, axis=-1, keepdims=True) + EPS)) * g


def _resident(shape):
    nd = len(shape)
    return pl.BlockSpec(shape, lambda *_: (0,) * nd)


def _inproj_body(x_ref, g_ref, w_ref, u_ref, q_ref, k_ref, v_ref, gp_ref, ga_ref, *, splits):
    xb = _rms(x_ref[...], g_ref[...]).astype(BF16)

    def proj(i):
        return jnp.dot(xb, w_ref[:, splits[i]:splits[i + 1]], preferred_element_type=F32)

    u_ref[...] = proj(0)
    q_ref[...] = (proj(1) * (HEAD_DIM ** -0.5)).astype(BF16)
    k_ref[...] = proj(2)
    v_ref[...] = proj(3)
    gp_ref[...] = jax.nn.sigmoid(proj(4))
    ga_ref[...] = jax.nn.sigmoid(proj(5))


def _inproj(x, g, w_in_bf):
    n, d = x.shape
    pw = 512
    splits = (0, pw, 2 * pw, 3 * pw, 4 * pw, 4 * pw + d, 4 * pw + 2 * d)
    tm = TOK_TILE
    row = lambda c: pl.BlockSpec((tm, c), lambda i: (i, 0))
    return pl.pallas_call(
        functools.partial(_inproj_body, splits=splits),
        grid=(n // tm,),
        in_specs=[row(d), _resident((1, d)), _resident(w_in_bf.shape)],
        out_specs=[row(pw), row(pw), row(pw), row(pw), row(d), row(d)],
        out_shape=[jax.ShapeDtypeStruct((n, pw), F32), jax.ShapeDtypeStruct((n, pw), BF16),
                   jax.ShapeDtypeStruct((n, pw), F32), jax.ShapeDtypeStruct((n, pw), F32),
                   jax.ShapeDtypeStruct((n, d), F32), jax.ShapeDtypeStruct((n, d), F32)],
        compiler_params=_params("parallel"),
    )(x, g, w_in_bf)


def _pool_body(hist_ref, u_ref, gw_ref, sc_ref, o_ref, buf, *, length, pos0):
    buf[0:HIST_ROWS, :] = hist_ref[...]
    buf[HIST_ROWS:HIST_ROWS + length, :] = u_ref[...]
    t = lax.broadcasted_iota(I32, (length, 1), 0)
    for gi, w in enumerate(POOL_WINDOWS):
        c0 = gi * LANES
        ug = buf[HIST_ROWS:HIST_ROWS + length, c0:c0 + LANES]
        acc = ug
        for j in range(1, w):
            acc = acc + buf[HIST_ROWS - j:HIST_ROWS - j + length, c0:c0 + LANES]
        cnt = jnp.minimum(t + (pos0 + 1), w).astype(F32)
        mixed = acc / cnt - ug
        y = jnp.dot(mixed.astype(BF16), gw_ref[gi], preferred_element_type=F32)
        o_ref[:, c0:c0 + LANES] = y * sc_ref[:, c0:c0 + LANES]


def _pool(hist, u, gw_bf, scale, pos0):
    b, length, pw = u.shape
    return pl.pallas_call(
        functools.partial(_pool_body, length=length, pos0=pos0),
        grid=(b,),
        in_specs=[pl.BlockSpec((None, HIST_ROWS, pw), lambda i: (i, 0, 0)),
                  pl.BlockSpec((None, length, pw), lambda i: (i, 0, 0)),
                  _resident(gw_bf.shape), _resident((1, pw))],
        out_specs=pl.BlockSpec((None, length, pw), lambda i: (i, 0, 0)),
        out_shape=jax.ShapeDtypeStruct((b, length, pw), F32),
        scratch_shapes=[pltpu.VMEM((HIST_ROWS + length, pw), F32)],
        compiler_params=_params("parallel"),
    )(hist, u, gw_bf, scale)


def _lambda(lq1, lk1, lq2, lk2, lam_init):
    d1 = jnp.sum(lq1[...] * lk1[...], axis=-1, keepdims=True)
    d2 = jnp.sum(lq2[...] * lk2[...], axis=-1, keepdims=True)
    return jnp.exp(d1) - jnp.exp(d2) + lam_init


def _softmax_update(s, vals, m, l, acc):
    mn = jnp.maximum(m, jnp.max(s, axis=-1, keepdims=True))
    alpha = jnp.exp(m - mn)
    p = jnp.exp(s - mn)
    l = alpha * l + jnp.sum(p, axis=-1, keepdims=True)
    pb = p.astype(BF16)
    pv = None
    for (c0, c1), vb in vals:
        term = jnp.dot(pb[:, c0:c1], vb, preferred_element_type=F32)
        pv = term if pv is None else pv + term
    return mn, l, alpha * acc + pv


def _head_norm(o, sg, lam_init):
    return _rms(o, sg) * (1.0 - lam_init)


def _t5_bucket(n):
    max_exact = N_BUCKETS // 2
    nf = jnp.maximum(n, 1).astype(F32)
    large = max_exact + (jnp.log(nf / max_exact) / math.log(MAX_DISTANCE / max_exact)
                         * (N_BUCKETS - max_exact)).astype(I32)
    large = jnp.minimum(large, N_BUCKETS - 1)
    return jnp.where(n < max_exact, n, large)


def _rel_bias(rel_bias, n):
    bucket = _t5_bucket(jnp.maximum(n, 0))
    table = rel_bias.astype(F32)
    b = jnp.zeros((table.shape[1],) + n.shape, F32)
    for i in range(N_BUCKETS):
        b = jnp.where(bucket == i, table[i].reshape((-1,) + (1,) * n.ndim), b)
    return jnp.where(n >= 0, b, NEG_INF)


def _pattn_body(q_ref, k_ref, v_ref, bias_ref, lq1, lk1, lq2, lk2, sg_ref, o_ref, kb, vb,
                *, blk, lam_init):
    i = pl.program_id(2)

    @pl.when(i == 0)
    def _():
        kb[...] = k_ref[...].astype(BF16)
        vb[...] = v_ref[...].astype(BF16)

    q = q_ref[...]
    lane = lax.broadcasted_iota(I32, q.shape, 1)
    zero = jnp.zeros_like(q)
    q1 = jnp.where(lane < HEAD_DIM, q, zero)
    q2 = jnp.where(lane < HEAD_DIM, zero, q)

    def step(j, carry):
        m1, l1, a1, m2, l2, a2 = carry
        off = pl.multiple_of(j * blk, blk)
        kc = kb[pl.ds(off, blk), :]
        vc = vb[pl.ds(off, blk), :]
        bt = bias_ref[jnp.minimum(i - j, 2)]
        s1 = lax.dot_general(q1, kc, _NT, preferred_element_type=F32) + bt
        s2 = lax.dot_general(q2, kc, _NT, preferred_element_type=F32) + bt
        vals = [((0, blk), vc)]
        m1, l1, a1 = _softmax_update(s1, vals, m1, l1, a1)
        m2, l2, a2 = _softmax_update(s2, vals, m2, l2, a2)
        return m1, l1, a1, m2, l2, a2

    col = lambda v: jnp.full((blk, 1), v, F32)
    acc0 = jnp.zeros((blk, V_DIM), F32)
    init = (col(-jnp.inf), col(0.0), acc0, col(-jnp.inf), col(0.0), acc0)
    _, l1, a1, _, l2, a2 = lax.fori_loop(0, i + 1, step, init)
    lam = _lambda(lq1, lk1, lq2, lk2, lam_init)
    o = a1 / l1 - lam * (a2 / l2)
    o_ref[...] = _head_norm(o, sg_ref[...], lam_init)


def _prompt_attention(q, k, v, bias_tiles, lams, sg, batch, seq, lam_init):
    blk = ATT_BLOCK
    nq = seq // blk
    hw = V_DIM
    lam_spec = _resident((1, HEAD_DIM))
    return pl.pallas_call(
        functools.partial(_pattn_body, blk=blk, lam_init=lam_init),
        grid=(batch, N_HEADS, nq),
        in_specs=[pl.BlockSpec((blk, hw), lambda b, h, i: (b * nq + i, h)),
                  pl.BlockSpec((seq, hw), lambda b, h, i: (b, h)),
                  pl.BlockSpec((seq, hw), lambda b, h, i: (b, h)),
                  pl.BlockSpec((None, 3, blk, blk), lambda b, h, i: (h, 0, 0, 0)),
                  lam_spec, lam_spec, lam_spec, lam_spec, _resident((1, hw))],
        out_specs=pl.BlockSpec((blk, hw), lambda b, h, i: (b * nq + i, h)),
        out_shape=jax.ShapeDtypeStruct((batch * seq, N_HEADS * hw), F32),
        scratch_shapes=[pltpu.VMEM((seq, hw), BF16), pltpu.VMEM((seq, hw), BF16)],
        compiler_params=_params("parallel", "parallel", "arbitrary"),
    )(q, k, v, bias_tiles, *lams, sg)


def _sattn_body(pt_ref, q_ref, bias_ref, bnew_ref, knew_ref, vnew_ref, lq1, lk1, lq2, lk2, sg_ref,
                *rest, pages, t_new, psz, lam_init):
    k_refs = rest[:pages]
    v_refs = rest[pages:2 * pages]
    o_ref, m_s, l_s, acc_s = rest[2 * pages:]
    j = pl.program_id(1)
    hr = 2 * t_new

    @pl.when(j == 0)
    def _():
        m_s[...] = jnp.full(m_s.shape, -jnp.inf, F32)
        l_s[...] = jnp.zeros(l_s.shape, F32)
        acc_s[...] = jnp.zeros(acc_s.shape, F32)

    def head_rows(ref, h):
        return ref[pl.ds(h, psz, stride=N_HEADS), :].astype(BF16)

    def update(k_pages, v_pages, bias):
        s = jnp.concatenate(
            [jnp.concatenate([lax.dot_general(q_ref[h], head_rows(kp, h), _NT, preferred_element_type=F32)
                              for kp in k_pages], axis=1) for h in range(N_HEADS)], axis=0) + bias
        m = m_s[...]
        mn = jnp.maximum(m, jnp.max(s, axis=-1, keepdims=True))
        alpha = jnp.exp(m - mn)
        p = jnp.exp(s - mn)
        m_s[...] = mn
        l_s[...] = alpha * l_s[...] + jnp.sum(p, axis=-1, keepdims=True)
        pv = []
        for h in range(N_HEADS):
            ph = p[h * hr:(h + 1) * hr]
            terms = [jnp.dot(ph[:, i * psz:(i + 1) * psz].astype(BF16), head_rows(vp, h),
                             preferred_element_type=F32) for i, vp in enumerate(v_pages)]
            pv.append(functools.reduce(lambda a, b: a + b, terms))
        acc_s[...] = alpha * acc_s[...] + jnp.concatenate(pv, axis=0)

    update(k_refs, v_refs, bias_ref[j])

    @pl.when(j == pl.num_programs(1) - 1)
    def _():
        update([knew_ref], [vnew_ref], bnew_ref[...])
        lam = _lambda(lq1, lk1, lq2, lk2, lam_init)
        o = acc_s[...] / l_s[...]
        for h in range(N_HEADS):
            r0 = h * hr
            od = o[r0:r0 + t_new] - lam * o[r0 + t_new:r0 + hr]
            o_ref[:, h * V_DIM:(h + 1) * V_DIM] = _head_norm(od, sg_ref[...], lam_init)


def _sample_attention(page_table, q_heads, bias_past, bias_new, k_new, v_new, cache_k, cache_v, lams, sg,
                      t_new, lam_init):
    b, n_pages = page_table.shape
    pages = PAGES_PER_STEP
    nj = n_pages // pages
    page_rows = cache_k.shape[1]
    psz = page_rows // N_HEADS
    rows = 2 * N_HEADS * t_new
    lam_spec = _resident((1, HEAD_DIM))

    def page_spec(p):
        return pl.BlockSpec((None, page_rows, V_DIM), lambda i, j, pt: (pt[i, j * pages + p], 0, 0))

    new_spec = pl.BlockSpec((None, page_rows, V_DIM), lambda i, j, pt: (i, 0, 0))
    grid_spec = pltpu.PrefetchScalarGridSpec(
        num_scalar_prefetch=1,
        grid=(b, nj),
        in_specs=[pl.BlockSpec((None, N_HEADS, 2 * t_new, V_DIM), lambda i, j, pt: (i, 0, 0, 0)),
                  _resident(bias_past.shape), _resident(bias_new.shape), new_spec, new_spec,
                  lam_spec, lam_spec, lam_spec, lam_spec, _resident((1, V_DIM))]
                 + [page_spec(p) for p in range(pages)] + [page_spec(p) for p in range(pages)],
        out_specs=pl.BlockSpec((None, t_new, N_HEADS * V_DIM), lambda i, j, pt: (i, 0, 0)),
        scratch_shapes=[pltpu.VMEM((rows, 1), F32), pltpu.VMEM((rows, 1), F32),
                        pltpu.VMEM((rows, V_DIM), F32)],
    )
    return pl.pallas_call(
        functools.partial(_sattn_body, pages=pages, t_new=t_new, psz=psz, lam_init=lam_init),
        grid_spec=grid_spec,
        out_shape=jax.ShapeDtypeStruct((b, t_new, N_HEADS * V_DIM), F32),
        compiler_params=_params("parallel", "arbitrary"),
    )(page_table, q_heads, bias_past, bias_new, k_new, v_new, *lams, sg,
      *([cache_k] * pages), *([cache_v] * pages))


def _merge_body(x_ref, po_ref, ao_ref, gp_ref, ga_ref, wbp_ref, wba_ref, wout_ref, h_ref):
    a = jnp.dot(po_ref[...].astype(BF16), wbp_ref[...], preferred_element_type=F32)
    b = jnp.dot(ao_ref[...].astype(BF16), wba_ref[...], preferred_element_type=F32)
    m = gp_ref[...] * a + ga_ref[...] * b
    h_ref[...] = x_ref[...] + jnp.dot(m.astype(BF16), wout_ref[...], preferred_element_type=F32)


def _merge(x, pool_o, attn_o, gp, ga, wbp, wba, wout):
    n, d = x.shape
    tm = TOK_TILE
    row = lambda c: pl.BlockSpec((tm, c), lambda i: (i, 0))
    return pl.pallas_call(
        _merge_body,
        grid=(n // tm,),
        in_specs=[row(d), row(pool_o.shape[1]), row(attn_o.shape[1]), row(d), row(d),
                  _resident(wbp.shape), _resident(wba.shape), _resident(wout.shape)],
        out_specs=row(d),
        out_shape=jax.ShapeDtypeStruct((n, d), F32),
        compiler_params=_params("parallel"),
    )(x, pool_o, attn_o, gp, ga, wbp, wba, wout)


def _extract_topk(x, key, k):
    big = jnp.iinfo(jnp.int32).max
    vals, keys = [], []
    for _ in range(k):
        m = jnp.max(x, axis=0, keepdims=True)
        kmin = jnp.min(jnp.where(x == m, key, big), axis=0, keepdims=True)
        x = jnp.where(key == kmin, -jnp.inf, x)
        vals.append(m)
        keys.append(kmin)
    return jnp.concatenate(vals, axis=0), jnp.concatenate(keys, axis=0)


def _route_body(h_ref, g_ref, wq_ref, sk_ref, xn_ref, eid_ref, gate_ref, q_scr, sv_scr, si_scr,
                eid_scr, gate_scr, *, tm):
    xn = _rms(h_ref[...], g_ref[...])
    xn_ref[...] = xn
    q = jnp.dot(xn.astype(BF16), wq_ref[...], preferred_element_type=F32)
    n_lists = 2 * PEER_HEADS
    for l in range(n_lists):
        q_scr[l] = q[:, l * PEER_HALF:(l + 1) * PEER_HALF].astype(BF16)
    groups = tm // LANES
    key_iota = lax.broadcasted_iota(I32, (N_KEYS, LANES), 0)

    def list_body(l, carry):
        st = lax.dot_general(sk_ref[l], q_scr[l], _NT, preferred_element_type=F32)
        for g in range(groups):
            cs = slice(g * LANES, (g + 1) * LANES)
            vals, idx = _extract_topk(st[:, cs], key_iota, PEER_TOPK)
            sv_scr[l, :, cs] = vals
            si_scr[l, :, cs] = idx
        return carry

    lax.fori_loop(0, n_lists, list_body, 0)

    j_iota = lax.broadcasted_iota(I32, (PEER_TOPK, LANES), 0)

    def head_body(hh, carry):
        rows = pl.ds(pl.multiple_of(hh * PEER_TOPK, PEER_TOPK), PEER_TOPK)
        for g in range(groups):
            cs = slice(g * LANES, (g + 1) * LANES)
            a = sv_scr[2 * hh, :, cs]
            b = sv_scr[2 * hh + 1, :, cs]
            ia = si_scr[2 * hh, :, cs]
            ib = si_scr[2 * hh + 1, :, cs]
            kb = (j_iota << KEY_SHIFT) + ib
            row_key = lambda i: ia[i:i + 1] * N_KEYS + ((i * PEER_TOPK) << KEY_SHIFT)
            half = PEER_TOPK // 2
            cands = [a[0:1] + b]
            keys = [kb + row_key(0)]
            for i in range(1, half):
                valid = j_iota[:half] < PEER_TOPK // (i + 1)
                cands.append(jnp.where(valid, a[i:i + 1] + b[:half], -jnp.inf))
                keys.append(kb[:half] + row_key(i))
            cands.append(a[half:] + b[0:1])
            keys.append(((j_iota[half:] * PEER_TOPK) << KEY_SHIFT) + ia[half:] * N_KEYS + ib[0:1])
            top_s, top_key = _extract_topk(jnp.concatenate(cands, axis=0), jnp.concatenate(keys, axis=0),
                                           PEER_TOPK)
            e = jnp.exp(top_s - top_s[0:1])
            eid_scr[rows, cs] = (top_key & ((1 << KEY_SHIFT) - 1)) * ROW_WORDS
            gate_scr[rows, cs] = e / jnp.sum(e, axis=0, keepdims=True)
        return carry

    lax.fori_loop(0, PEER_HEADS, head_body, 0)
    eid_ref[...] = eid_scr[...].T
    gate_ref[...] = gate_scr[...].T


def _route(h, g, wq_bf, sk_bf):
    n, d = h.shape
    tm = TOK_TILE
    n_lists = 2 * PEER_HEADS
    row = lambda c: pl.BlockSpec((tm, c), lambda i: (i, 0))
    return pl.pallas_call(
        functools.partial(_route_body, tm=tm),
        grid=(n // tm,),
        in_specs=[row(d), _resident((1, d)), _resident(wq_bf.shape), _resident(sk_bf.shape)],
        out_specs=[row(d), row(N_SLOTS), row(N_SLOTS)],
        out_shape=[jax.ShapeDtypeStruct((n, d), F32), jax.ShapeDtypeStruct((n, N_SLOTS), I32),
                   jax.ShapeDtypeStruct((n, N_SLOTS), F32)],
        scratch_shapes=[pltpu.VMEM((n_lists, tm, PEER_HALF), BF16),
                        pltpu.VMEM((n_lists, PEER_TOPK, tm), F32),
                        pltpu.VMEM((n_lists, PEER_TOPK, tm), I32),
                        pltpu.VMEM((N_SLOTS, tm), I32),
                        pltpu.VMEM((N_SLOTS, tm), F32)],
        compiler_params=_params("parallel"),
    )(h, g, wq_bf, sk_bf)


CHUNKS = 2 * ROW_WORDS
GB_ROWS = N_SLOTS * CHUNKS
N_GBUF = 2
TOKENS_PER_ITER = 8
_GBUFS = [pltpu.VMEM((N_SLOTS * ROW_WORDS, LANES), I32)] * N_GBUF


def _pack_table(tab):
    n, d = tab.shape
    bits = lax.bitcast_convert_type(tab.astype(BF16), jnp.uint16).astype(jnp.uint32)
    bits = bits.reshape(n, ROW_WORDS, 2, LANES)
    words = (bits[:, :, 1, :] << 16) | bits[:, :, 0, :]
    return lax.bitcast_convert_type(words, I32).reshape(n * ROW_WORDS, LANES)


def _stage_rows(eid_ref, t, tab_ref, gbuf):
    for r in range(N_SLOTS):
        off = pl.multiple_of(eid_ref[t, r], ROW_WORDS)
        gbuf[r * ROW_WORDS:(r + 1) * ROW_WORDS, :] = tab_ref[pl.ds(off, ROW_WORDS), :]


def _staged(gbuf):
    return pltpu.bitcast(gbuf[...], BF16)


def _for_each_token(tb, eid_ref, tab_ref, gbufs, start, finish):
    _stage_rows(eid_ref, 0, tab_ref, gbufs[0])

    def group(i, carry):
        t0 = TOKENS_PER_ITER * i
        for j in range(TOKENS_PER_ITER):
            y = start(t0 + j, _staged(gbufs[j % 2]))
            _stage_rows(eid_ref, jnp.minimum(t0 + j + 1, tb - 1), tab_ref, gbufs[(j + 1) % 2])
            finish(t0 + j, y)
        return carry

    lax.fori_loop(0, tb // TOKENS_PER_ITER, group, 0)


def _chunk_mask():
    k = lax.broadcasted_iota(I32, (CHUNKS, GB_ROWS), 1)
    q = lax.broadcasted_iota(I32, (CHUNKS, GB_ROWS), 0)
    return (k & (CHUNKS - 1)) == q


def _token_rows(t):
    return pl.ds(pl.multiple_of(t * CHUNKS, CHUNKS), CHUNKS)


def _upass_body(eid_ref, xn_ref, gate_ref, tab_ref, rexp_ref, rsum_ref, w_ref, *scratch, tb):
    gbufs, (xbuf, zbuf) = scratch[:N_GBUF], scratch[N_GBUF:]
    for q in range(CHUNKS):
        xbuf[pl.ds(q, tb, stride=CHUNKS), :] = xn_ref[:, q * LANES:(q + 1) * LANES]
    mask = _chunk_mask()

    def start(t, g):
        x8 = xbuf[_token_rows(t), :].astype(BF16)
        return lax.dot_general(x8, g, _NT, preferred_element_type=F32)

    def finish(t, y):
        zbuf[pl.ds(t, 1), :] = jnp.sum(jnp.where(mask, y, 0.0), axis=0, keepdims=True)

    _for_each_token(tb, eid_ref, tab_ref, gbufs, start, finish)
    a = jnp.dot(zbuf[...], rsum_ref[...], preferred_element_type=F32, precision=lax.Precision.HIGHEST)
    act = 0.5 * a * (1.0 + lax.erf(a * math.sqrt(0.5)))
    w = (gate_ref[...] * act).astype(BF16)
    w_ref[...] = jnp.dot(w, rexp_ref[...], preferred_element_type=F32)


def _table_spec(tab):
    return pl.BlockSpec(tab.shape, lambda i: (0, 0), pipeline_mode=pl.Buffered(1))


def _gather_specs(d):
    tb = GATHER_TILE
    smem = pl.BlockSpec((tb, N_SLOTS), lambda i: (i, 0), memory_space=pltpu.SMEM)
    return tb, smem, (lambda c: pl.BlockSpec((tb, c), lambda i: (i, 0)))


def _upass(eid, xn, gate, tab, rexp, rsum):
    n, d = xn.shape
    tb, smem, row = _gather_specs(d)
    return pl.pallas_call(
        functools.partial(_upass_body, tb=tb),
        grid=(n // tb,),
        in_specs=[smem, row(d), row(N_SLOTS), _table_spec(tab), _resident(rexp.shape), _resident(rsum.shape)],
        out_specs=row(GB_ROWS),
        out_shape=jax.ShapeDtypeStruct((n, GB_ROWS), F32),
        scratch_shapes=_GBUFS + [
                        pltpu.VMEM((tb * CHUNKS, LANES), F32),
                        pltpu.VMEM((tb, GB_ROWS), F32)],
        compiler_params=_params("parallel"),
    )(eid, xn, gate, tab, rexp, rsum)


def _vpass_body(eid_ref, w_ref, h_ref, g_ref, tab_ref, y_ref, *scratch, tb):
    gbufs, pbuf = scratch[:N_GBUF], scratch[N_GBUF]
    mask = _chunk_mask()

    def start(t, g):
        wsel = jnp.where(mask, w_ref[pl.ds(t, 1), :], 0.0).astype(BF16)
        return jnp.dot(wsel, g, preferred_element_type=F32)

    def finish(t, y):
        pbuf[_token_rows(t), :] = y

    _for_each_token(tb, eid_ref, tab_ref, gbufs, start, finish)
    peer = jnp.concatenate([pbuf[pl.ds(q, tb, stride=CHUNKS), :] for q in range(CHUNKS)], axis=1)
    y_ref[...] = _rms(h_ref[...] + peer, g_ref[...])


def _vpass(eid, w, h, gfin, tab):
    n, d = h.shape
    tb, smem, row = _gather_specs(d)
    return pl.pallas_call(
        functools.partial(_vpass_body, tb=tb),
        grid=(n // tb,),
        in_specs=[smem, row(GB_ROWS), row(d), _resident((1, d)), _table_spec(tab)],
        out_specs=row(d),
        out_shape=jax.ShapeDtypeStruct((n, d), F32),
        scratch_shapes=_GBUFS + [pltpu.VMEM((tb * CHUNKS, LANES), F32)],
        compiler_params=_params("parallel"),
    )(eid, w, h, gfin, tab)


def _peer_and_final_norm(h, g_ffn, wq_bf, sk_bf, u_tab, v_tab, g_final):
    xn, eid, gate = _route(h, g_ffn, wq_bf, sk_bf)
    slot_of_row = jnp.arange(GB_ROWS, dtype=I32) // CHUNKS
    rexp = (jnp.arange(N_SLOTS, dtype=I32)[:, None] == slot_of_row[None, :])
    w = _upass(eid, xn, gate, u_tab, rexp.astype(BF16), rexp.T.astype(F32))
    return _vpass(eid, w, h, g_final.reshape(1, -1), v_tab)


def kernel(x_prompt, x_sample, cache_k, cache_v, state_pool, page_table, norm_mix_g, w_in, pool_group_w,
           pool_scale, lambda_q1, lambda_k1, lambda_q2, lambda_k2, subln_g, rel_bias, w_branch_pool,
           w_branch_attn, w_out, norm_ffn_g, peer_wq, peer_subkeys, peer_u, peer_v, norm_final_g):
    depth = w_in.shape[0]
    assert depth == 1
    b_p, s_p, d = x_prompt.shape
    b_s, t_s, _ = x_sample.shape
    psz = cache_k.shape[2]
    past = page_table.shape[1] * psz
    width = N_HEADS * V_DIM
    assert ATT_BLOCK >= MAX_DISTANCE and s_p % ATT_BLOCK == 0
    l = 0
    lam_init = 0.8 - 0.6 * math.exp(-0.3 * l)

    w_in_bf = w_in[l].astype(BF16)
    gw_bf = pool_group_w[l].astype(BF16)
    wbp, wba, wout = (w[l].astype(BF16) for w in (w_branch_pool, w_branch_attn, w_out))
    wq_bf = peer_wq[l].astype(BF16)
    sk_bf = peer_subkeys[l].astype(BF16).reshape(2 * PEER_HEADS, N_KEYS, PEER_HALF)
    u_tab = _pack_table(peer_u.reshape(-1, d))
    v_tab = _pack_table(peer_v.reshape(-1, d))
    g_mix = norm_mix_g[l].reshape(1, d)
    g_ffn = norm_ffn_g[l].reshape(1, d)
    scale = pool_scale[l].reshape(1, -1)
    sg = subln_g[l].reshape(1, V_DIM)
    lams = tuple(a[l].astype(F32).reshape(1, HEAD_DIM) for a in (lambda_q1, lambda_k1, lambda_q2, lambda_k2))

    r = jnp.arange(ATT_BLOCK, dtype=I32)
    dist = jnp.stack([dj * ATT_BLOCK + r[:, None] - r[None, :] for dj in range(3)])
    bias_tiles = _rel_bias(rel_bias, dist)

    tq = jnp.arange(t_s, dtype=I32)
    dist_past = past + tq[:, None] - jnp.arange(past, dtype=I32)[None, :]
    rows = 2 * N_HEADS * t_s
    per_map = lambda b: jnp.broadcast_to(b[:, None], (N_HEADS, 2) + b.shape[1:]).reshape(rows, -1)
    step_keys = PAGES_PER_STEP * psz
    bias_past = per_map(_rel_bias(rel_bias, dist_past))
    bias_past = bias_past.reshape(rows, past // step_keys, step_keys).transpose(1, 0, 2)
    dist_new = tq[:, None] - jnp.arange(psz, dtype=I32)[None, :]
    dist_new = jnp.where(jnp.arange(psz)[None, :] < t_s, dist_new, -1)
    bias_new = per_map(_rel_bias(rel_bias, dist_new))

    def token_stages(x, hist, pos0, attention):
        b, length, _ = x.shape
        n = b * length
        u, q, k, v, gp, ga = _inproj(x.reshape(n, d), g_mix, w_in_bf)
        pool_o = _pool(hist, u.reshape(b, length, -1), gw_bf, scale, pos0)
        attn_o = attention(q, k, v)
        h = _merge(x.reshape(n, d), pool_o.reshape(n, -1), attn_o, gp, ga, wbp, wba, wout)
        y = _peer_and_final_norm(h, g_ffn, wq_bf, sk_bf, u_tab, v_tab, norm_final_g)
        return y.reshape(b, length, d), u, k, v

    hist_p = jnp.zeros((b_p, HIST_ROWS, width), F32)
    attn_p = lambda q, k, v: _prompt_attention(q, k, v, bias_tiles, lams, sg, b_p, s_p, lam_init)
    y_p, u_p, k_p, v_p = token_stages(x_prompt, hist_p, 0, attn_p)

    hist_s = jnp.pad(state_pool[l], ((0, 0), (HIST_ROWS - POOL_STATE, 0), (0, 0)))
    ck = cache_k.reshape(-1, psz * N_HEADS, V_DIM)
    cv = cache_v.reshape(-1, psz * N_HEADS, V_DIM)

    def attn_s(q, k, v):
        q4 = q.reshape(b_s, t_s, N_HEADS, V_DIM).transpose(0, 2, 1, 3)
        first = jnp.arange(V_DIM) < HEAD_DIM
        zero = jnp.zeros((), BF16)
        q_heads = jnp.concatenate([jnp.where(first, q4, zero), jnp.where(first, zero, q4)], axis=2)
        pad = ((0, 0), (0, (psz - t_s) * N_HEADS), (0, 0))
        k_new = jnp.pad(k.reshape(b_s, t_s * N_HEADS, V_DIM), pad)
        v_new = jnp.pad(v.reshape(b_s, t_s * N_HEADS, V_DIM), pad)
        o = _sample_attention(page_table, q_heads, bias_past, bias_new, k_new, v_new, ck, cv, lams, sg,
                              t_s, lam_init)
        return o.reshape(b_s * t_s, width)

    y_s, u_s, k_s, v_s = token_stages(x_sample, hist_s, past, attn_s)

    kv_shape = lambda b, length: (1, b, length, N_HEADS, V_DIM)
    u_p3 = u_p.reshape(b_p, s_p, width)
    u_s3 = u_s.reshape(b_s, t_s, width)
    pool_prompt = u_p3[:, -POOL_STATE:][None]
    pool_sample = jnp.concatenate([state_pool[l], u_s3], axis=1)[:, -POOL_STATE:][None]
    return (y_p, y_s, k_p.reshape(kv_shape(b_p, s_p)), v_p.reshape(kv_shape(b_p, s_p)), pool_prompt,
            k_s.reshape(kv_shape(b_s, t_s)), v_s.reshape(kv_shape(b_s, t_s)), pool_sample)
```

```python
import functools
import math

import jax
import jax.numpy as jnp
from jax import lax
from jax.experimental import pallas as pl
from jax.experimental.pallas import tpu as pltpu

F32 = jnp.float32
BF16 = jnp.bfloat16
I32 = jnp.int32

EPS = 1e-6
NEG_INF = -1e30

LANES = 128
SUBLANES = 8
VMEM_LIMIT_BYTES = 56 * 1024 * 1024

POOL_WINDOWS = (2, 4, 8, 16)
POOL_STATE = max(POOL_WINDOWS) - 1
HIST_ROWS = 16
N_HEADS = 4
HEAD_DIM = 64
V_DIM = 2 * HEAD_DIM
N_BUCKETS = 32
MAX_DISTANCE = 128
PEER_HEADS = 8
PEER_TOPK = 16
N_KEYS = 128
PEER_HALF = 128
N_SLOTS = PEER_HEADS * PEER_TOPK
ROW_WORDS = 4
KEY_SHIFT = 14

TOK_TILE = 256
ATT_BLOCK = 256
PAGES_PER_STEP = 16
GATHER_TILE = 128

_NT = (((1,), (1,)), ((), ()))


def _params(*sem):
    return pltpu.CompilerParams(dimension_semantics=sem, vmem_limit_bytes=VMEM_LIMIT_BYTES)


def _rms(x, g):
    return (x * lax.rsqrt(jnp.mean(x * x, axis=-1, keepdims=True) + EPS)) * g


def _resident(shape):
    nd = len(shape)
    return pl.BlockSpec(shape, lambda *_: (0,) * nd)


def _inproj_body(x_ref, g_ref, w_ref, u_ref, q_ref, k_ref, v_ref, gp_ref, ga_ref, *, splits):
    xb = _rms(x_ref[...], g_ref[...]).astype(BF16)

    def proj(i):
        return jnp.dot(xb, w_ref[:, splits[i]:splits[i + 1]], preferred_element_type=F32)

    u_ref[...] = proj(0)
    q_ref[...] = (proj(1) * (HEAD_DIM ** -0.5)).astype(BF16)
    k_ref[...] = proj(2)
    v_ref[...] = proj(3)
    gp_ref[...] = jax.nn.sigmoid(proj(4))
    ga_ref[...] = jax.nn.sigmoid(proj(5))


def _inproj(x, g, w_in_bf):
    n, d = x.shape
    pw = 512
    splits = (0, pw, 2 * pw, 3 * pw, 4 * pw, 4 * pw + d, 4 * pw + 2 * d)
    tm = TOK_TILE
    row = lambda c: pl.BlockSpec((tm, c), lambda i: (i, 0))
    return pl.pallas_call(
        functools.partial(_inproj_body, splits=splits),
        grid=(n // tm,),
        in_specs=[row(d), _resident((1, d)), _resident(w_in_bf.shape)],
        out_specs=[row(pw), row(pw), row(pw), row(pw), row(d), row(d)],
        out_shape=[jax.ShapeDtypeStruct((n, pw), F32), jax.ShapeDtypeStruct((n, pw), BF16),
                   jax.ShapeDtypeStruct((n, pw), F32), jax.ShapeDtypeStruct((n, pw), F32),
                   jax.ShapeDtypeStruct((n, d), F32), jax.ShapeDtypeStruct((n, d), F32)],
        compiler_params=_params("parallel"),
    )(x, g, w_in_bf)


def _pool_body(hist_ref, u_ref, gw_ref, sc_ref, o_ref, buf, *, length, pos0):
    buf[0:HIST_ROWS, :] = hist_ref[...]
    buf[HIST_ROWS:HIST_ROWS + length, :] = u_ref[...]
    t = lax.broadcasted_iota(I32, (length, 1), 0)
    for gi, w in enumerate(POOL_WINDOWS):
        c0 = gi * LANES
        ug = buf[HIST_ROWS:HIST_ROWS + length, c0:c0 + LANES]
        acc = ug
        for j in range(1, w):
            acc = acc + buf[HIST_ROWS - j:HIST_ROWS - j + length, c0:c0 + LANES]
        cnt = jnp.minimum(t + (pos0 + 1), w).astype(F32)
        mixed = acc / cnt - ug
        y = jnp.dot(mixed.astype(BF16), gw_ref[gi], preferred_element_type=F32)
        o_ref[:, c0:c0 + LANES] = y * sc_ref[:, c0:c0 + LANES]


def _pool(hist, u, gw_bf, scale, pos0):
    b, length, pw = u.shape
    return pl.pallas_call(
        functools.partial(_pool_body, length=length, pos0=pos0),
        grid=(b,),
        in_specs=[pl.BlockSpec((None, HIST_ROWS, pw), lambda i: (i, 0, 0)),
                  pl.BlockSpec((None, length, pw), lambda i: (i, 0, 0)),
                  _resident(gw_bf.shape), _resident((1, pw))],
        out_specs=pl.BlockSpec((None, length, pw), lambda i: (i, 0, 0)),
        out_shape=jax.ShapeDtypeStruct((b, length, pw), F32),
        scratch_shapes=[pltpu.VMEM((HIST_ROWS + length, pw), F32)],
        compiler_params=_params("parallel"),
    )(hist, u, gw_bf, scale)


def _lambda(lq1, lk1, lq2, lk2, lam_init):
    d1 = jnp.sum(lq1[...] * lk1[...], axis=-1, keepdims=True)
    d2 = jnp.sum(lq2[...] * lk2[...], axis=-1, keepdims=True)
    return jnp.exp(d1) - jnp.exp(d2) + lam_init


def _softmax_update(s, vals, m, l, acc):
    mn = jnp.maximum(m, jnp.max(s, axis=-1, keepdims=True))
    alpha = jnp.exp(m - mn)
    p = jnp.exp(s - mn)
    l = alpha * l + jnp.sum(p, axis=-1, keepdims=True)
    pb = p.astype(BF16)
    pv = None
    for (c0, c1), vb in vals:
        term = jnp.dot(pb[:, c0:c1], vb, preferred_element_type=F32)
        pv = term if pv is None else pv + term
    return mn, l, alpha * acc + pv


def _head_norm(o, sg, lam_init):
    return _rms(o, sg) * (1.0 - lam_init)


def _t5_bucket(n):
    max_exact = N_BUCKETS // 2
    nf = jnp.maximum(n, 1).astype(F32)
    large = max_exact + (jnp.log(nf / max_exact) / math.log(MAX_DISTANCE / max_exact)
                         * (N_BUCKETS - max_exact)).astype(I32)
    large = jnp.minimum(large, N_BUCKETS - 1)
    return jnp.where(n < max_exact, n, large)


def _rel_bias(rel_bias, n):
    bucket = _t5_bucket(jnp.maximum(n, 0))
    table = rel_bias.astype(F32)
    b = jnp.zeros((table.shape[1],) + n.shape, F32)
    for i in range(N_BUCKETS):
        b = jnp.where(bucket == i, table[i].reshape((-1,) + (1,) * n.ndim), b)
    return jnp.where(n >= 0, b, NEG_INF)


def _pattn_body(q_ref, k_ref, v_ref, bias_ref, lq1, lk1, lq2, lk2, sg_ref, o_ref, kb, vb,
                *, blk, lam_init):
    i = pl.program_id(2)

    @pl.when(i == 0)
    def _():
        kb[...] = k_ref[...].astype(BF16)
        vb[...] = v_ref[...].astype(BF16)

    q = q_ref[...]
    lane = lax.broadcasted_iota(I32, q.shape, 1)
    zero = jnp.zeros_like(q)
    q1 = jnp.where(lane < HEAD_DIM, q, zero)
    q2 = jnp.where(lane < HEAD_DIM, zero, q)

    def step(j, carry):
        m1, l1, a1, m2, l2, a2 = carry
        off = pl.multiple_of(j * blk, blk)
        kc = kb[pl.ds(off, blk), :]
        vc = vb[pl.ds(off, blk), :]
        bt = bias_ref[jnp.minimum(i - j, 2)]
        s1 = lax.dot_general(q1, kc, _NT, preferred_element_type=F32) + bt
        s2 = lax.dot_general(q2, kc, _NT, preferred_element_type=F32) + bt
        vals = [((0, blk), vc)]
        m1, l1, a1 = _softmax_update(s1, vals, m1, l1, a1)
        m2, l2, a2 = _softmax_update(s2, vals, m2, l2, a2)
        return m1, l1, a1, m2, l2, a2

    col = lambda v: jnp.full((blk, 1), v, F32)
    acc0 = jnp.zeros((blk, V_DIM), F32)
    init = (col(-jnp.inf), col(0.0), acc0, col(-jnp.inf), col(0.0), acc0)
    _, l1, a1, _, l2, a2 = lax.fori_loop(0, i + 1, step, init)
    lam = _lambda(lq1, lk1, lq2, lk2, lam_init)
    o = a1 / l1 - lam * (a2 / l2)
    o_ref[...] = _head_norm(o, sg_ref[...], lam_init)


def _prompt_attention(q, k, v, bias_tiles, lams, sg, batch, seq, lam_init):
    blk = ATT_BLOCK
    nq = seq // blk
    hw = V_DIM
    lam_spec = _resident((1, HEAD_DIM))
    return pl.pallas_call(
        functools.partial(_pattn_body, blk=blk, lam_init=lam_init),
        grid=(batch, N_HEADS, nq),
        in_specs=[pl.BlockSpec((blk, hw), lambda b, h, i: (b * nq + i, h)),
                  pl.BlockSpec((seq, hw), lambda b, h, i: (b, h)),
                  pl.BlockSpec((seq, hw), lambda b, h, i: (b, h)),
                  pl.BlockSpec((None, 3, blk, blk), lambda b, h, i: (h, 0, 0, 0)),
                  lam_spec, lam_spec, lam_spec, lam_spec, _resident((1, hw))],
        out_specs=pl.BlockSpec((blk, hw), lambda b, h, i: (b * nq + i, h)),
        out_shape=jax.ShapeDtypeStruct((batch * seq, N_HEADS * hw), F32),
        scratch_shapes=[pltpu.VMEM((seq, hw), BF16), pltpu.VMEM((seq, hw), BF16)],
        compiler_params=_params("parallel", "parallel", "arbitrary"),
    )(q, k, v, bias_tiles, *lams, sg)


def _sattn_body(pt_ref, q_ref, bias_ref, bnew_ref, knew_ref, vnew_ref, lq1, lk1, lq2, lk2, sg_ref,
                *rest, pages, t_new, psz, lam_init):
    k_refs = rest[:pages]
    v_refs = rest[pages:2 * pages]
    o_ref, m_s, l_s, acc_s = rest[2 * pages:]
    j = pl.program_id(1)
    hr = 2 * t_new

    @pl.when(j == 0)
    def _():
        m_s[...] = jnp.full(m_s.shape, -jnp.inf, F32)
        l_s[...] = jnp.zeros(l_s.shape, F32)
        acc_s[...] = jnp.zeros(acc_s.shape, F32)

    def head_rows(ref, h):
        return ref[pl.ds(h, psz, stride=N_HEADS), :].astype(BF16)

    def update(k_pages, v_pages, bias):
        s = jnp.concatenate(
            [jnp.concatenate([lax.dot_general(q_ref[h], head_rows(kp, h), _NT, preferred_element_type=F32)
                              for kp in k_pages], axis=1) for h in range(N_HEADS)], axis=0) + bias
        m = m_s[...]
        mn = jnp.maximum(m, jnp.max(s, axis=-1, keepdims=True))
        alpha = jnp.exp(m - mn)
        p = jnp.exp(s - mn)
        m_s[...] = mn
        l_s[...] = alpha * l_s[...] + jnp.sum(p, axis=-1, keepdims=True)
        pv = []
        for h in range(N_HEADS):
            ph = p[h * hr:(h + 1) * hr]
            terms = [jnp.dot(ph[:, i * psz:(i + 1) * psz].astype(BF16), head_rows(vp, h),
                             preferred_element_type=F32) for i, vp in enumerate(v_pages)]
            pv.append(functools.reduce(lambda a, b: a + b, terms))
        acc_s[...] = alpha * acc_s[...] + jnp.concatenate(pv, axis=0)

    update(k_refs, v_refs, bias_ref[j])

    @pl.when(j == pl.num_programs(1) - 1)
    def _():
        update([knew_ref], [vnew_ref], bnew_ref[...])
        lam = _lambda(lq1, lk1, lq2, lk2, lam_init)
        o = acc_s[...] / l_s[...]
        for h in range(N_HEADS):
            r0 = h * hr
            od = o[r0:r0 + t_new] - lam * o[r0 + t_new:r0 + hr]
            o_ref[:, h * V_DIM:(h + 1) * V_DIM] = _head_norm(od, sg_ref[...], lam_init)


def _sample_attention(page_table, q_heads, bias_past, bias_new, k_new, v_new, cache_k, cache_v, lams, sg,
                      t_new, lam_init):
    b, n_pages = page_table.shape
    pages = PAGES_PER_STEP
    nj = n_pages // pages
    page_rows = cache_k.shape[1]
    psz = page_rows // N_HEADS
    rows = 2 * N_HEADS * t_new
    lam_spec = _resident((1, HEAD_DIM))

    def page_spec(p):
        return pl.BlockSpec((None, page_rows, V_DIM), lambda i, j, pt: (pt[i, j * pages + p], 0, 0))

    new_spec = pl.BlockSpec((None, page_rows, V_DIM), lambda i, j, pt: (i, 0, 0))
    grid_spec = pltpu.PrefetchScalarGridSpec(
        num_scalar_prefetch=1,
        grid=(b, nj),
        in_specs=[pl.BlockSpec((None, N_HEADS, 2 * t_new, V_DIM), lambda i, j, pt: (i, 0, 0, 0)),
                  _resident(bias_past.shape), _resident(bias_new.shape), new_spec, new_spec,
                  lam_spec, lam_spec, lam_spec, lam_spec, _resident((1, V_DIM))]
                 + [page_spec(p) for p in range(pages)] + [page_spec(p) for p in range(pages)],
        out_specs=pl.BlockSpec((None, t_new, N_HEADS * V_DIM), lambda i, j, pt: (i, 0, 0)),
        scratch_shapes=[pltpu.VMEM((rows, 1), F32), pltpu.VMEM((rows, 1), F32),
                        pltpu.VMEM((rows, V_DIM), F32)],
    )
    return pl.pallas_call(
        functools.partial(_sattn_body, pages=pages, t_new=t_new, psz=psz, lam_init=lam_init),
        grid_spec=grid_spec,
        out_shape=jax.ShapeDtypeStruct((b, t_new, N_HEADS * V_DIM), F32),
        compiler_params=_params("parallel", "arbitrary"),
    )(page_table, q_heads, bias_past, bias_new, k_new, v_new, *lams, sg,
      *([cache_k] * pages), *([cache_v] * pages))


def _merge_body(x_ref, po_ref, ao_ref, gp_ref, ga_ref, wbp_ref, wba_ref, wout_ref, h_ref):
    a = jnp.dot(po_ref[...].astype(BF16), wbp_ref[...], preferred_element_type=F32)
    b = jnp.dot(ao_ref[...].astype(BF16), wba_ref[...], preferred_element_type=F32)
    m = gp_ref[...] * a + ga_ref[...] * b
    h_ref[...] = x_ref[...] + jnp.dot(m.astype(BF16), wout_ref[...], preferred_element_type=F32)


def _merge(x, pool_o, attn_o, gp, ga, wbp, wba, wout):
    n, d = x.shape
    tm = TOK_TILE
    row = lambda c: pl.BlockSpec((tm, c), lambda i: (i, 0))
    return pl.pallas_call(
        _merge_body,
        grid=(n // tm,),
        in_specs=[row(d), row(pool_o.shape[1]), row(attn_o.shape[1]), row(d), row(d),
                  _resident(wbp.shape), _resident(wba.shape), _resident(wout.shape)],
        out_specs=row(d),
        out_shape=jax.ShapeDtypeStruct((n, d), F32),
        compiler_params=_params("parallel"),
    )(x, pool_o, attn_o, gp, ga, wbp, wba, wout)


def _extract_topk(x, key, k):
    big = jnp.iinfo(jnp.int32).max
    vals, keys = [], []
    for _ in range(k):
        m = jnp.max(x, axis=0, keepdims=True)
        kmin = jnp.min(jnp.where(x == m, key, big), axis=0, keepdims=True)
        x = jnp.where(key == kmin, -jnp.inf, x)
        vals.append(m)
        keys.append(kmin)
    return jnp.concatenate(vals, axis=0), jnp.concatenate(keys, axis=0)


def _route_body(h_ref, g_ref, wq_ref, sk_ref, xn_ref, eid_ref, gate_ref, q_scr, sv_scr, si_scr,
                eid_scr, gate_scr, *, tm):
    xn = _rms(h_ref[...], g_ref[...])
    xn_ref[...] = xn
    q = jnp.dot(xn.astype(BF16), wq_ref[...], preferred_element_type=F32)
    n_lists = 2 * PEER_HEADS
    for l in range(n_lists):
        q_scr[l] = q[:, l * PEER_HALF:(l + 1) * PEER_HALF].astype(BF16)
    groups = tm // LANES
    key_iota = lax.broadcasted_iota(I32, (N_KEYS, LANES), 0)

    def list_body(i, carry):
        for l in (2 * i, 2 * i + 1):
            st = lax.dot_general(sk_ref[l], q_scr[l], _NT, preferred_element_type=F32)
            for g in range(groups):
                cs = slice(g * LANES, (g + 1) * LANES)
                vals, idx = _extract_topk(st[:, cs], key_iota, PEER_TOPK)
                sv_scr[l, :, cs] = vals
                si_scr[l, :, cs] = idx
        return carry

    lax.fori_loop(0, n_lists // 2, list_body, 0)

    j_iota = lax.broadcasted_iota(I32, (PEER_TOPK, LANES), 0)

    def head_body(hh, carry):
        rows = pl.ds(pl.multiple_of(hh * PEER_TOPK, PEER_TOPK), PEER_TOPK)
        for g in range(groups):
            cs = slice(g * LANES, (g + 1) * LANES)
            a = sv_scr[2 * hh, :, cs]
            b = sv_scr[2 * hh + 1, :, cs]
            ia = si_scr[2 * hh, :, cs]
            ib = si_scr[2 * hh + 1, :, cs]
            kb = (j_iota << KEY_SHIFT) + ib
            row_key = lambda i: ia[i:i + 1] * N_KEYS + ((i * PEER_TOPK) << KEY_SHIFT)
            half = PEER_TOPK // 2
            cands = [a[0:1] + b]
            keys = [kb + row_key(0)]
            for i in range(1, half):
                valid = j_iota[:half] < PEER_TOPK // (i + 1)
                cands.append(jnp.where(valid, a[i:i + 1] + b[:half], -jnp.inf))
                keys.append(kb[:half] + row_key(i))
            cands.append(a[half:] + b[0:1])
            keys.append(((j_iota[half:] * PEER_TOPK) << KEY_SHIFT) + ia[half:] * N_KEYS + ib[0:1])
            top_s, top_key = _extract_topk(jnp.concatenate(cands, axis=0), jnp.concatenate(keys, axis=0),
                                           PEER_TOPK)
            e = jnp.exp(top_s - top_s[0:1])
            eid_scr[rows, cs] = (top_key & ((1 << KEY_SHIFT) - 1)) * ROW_WORDS
            gate_scr[rows, cs] = e / jnp.sum(e, axis=0, keepdims=True)
        return carry

    lax.fori_loop(0, PEER_HEADS, head_body, 0)
    eid_ref[...] = eid_scr[...].T
    gate_ref[...] = gate_scr[...].T


def _route(h, g, wq_bf, sk_bf):
    n, d = h.shape
    tm = TOK_TILE
    n_lists = 2 * PEER_HEADS
    row = lambda c: pl.BlockSpec((tm, c), lambda i: (i, 0))
    return pl.pallas_call(
        functools.partial(_route_body, tm=tm),
        grid=(n // tm,),
        in_specs=[row(d), _resident((1, d)), _resident(wq_bf.shape), _resident(sk_bf.shape)],
        out_specs=[row(d), row(N_SLOTS), row(N_SLOTS)],
        out_shape=[jax.ShapeDtypeStruct((n, d), F32), jax.ShapeDtypeStruct((n, N_SLOTS), I32),
                   jax.ShapeDtypeStruct((n, N_SLOTS), F32)],
        scratch_shapes=[pltpu.VMEM((n_lists, tm, PEER_HALF), BF16),
                        pltpu.VMEM((n_lists, PEER_TOPK, tm), F32),
                        pltpu.VMEM((n_lists, PEER_TOPK, tm), I32),
                        pltpu.VMEM((N_SLOTS, tm), I32),
                        pltpu.VMEM((N_SLOTS, tm), F32)],
        compiler_params=_params("parallel"),
    )(h, g, wq_bf, sk_bf)


CHUNKS = 2 * ROW_WORDS
GB_ROWS = N_SLOTS * CHUNKS
N_GBUF = 2
TOKENS_PER_ITER = 16
_GBUFS = [pltpu.VMEM((N_SLOTS * ROW_WORDS, LANES), I32)] * N_GBUF


PACK_TILE = 512


def _bf16_bits(x):
    b = lax.bitcast_convert_type(x, jnp.uint32)
    return (b + 0x7FFF + ((b >> 16) & 1)) & jnp.uint32(0xFFFF0000)


def _pack_body(t_ref, o_ref, *, rows):
    for s in range(ROW_WORDS):
        lo = _bf16_bits(t_ref[:, (2 * s) * LANES:(2 * s + 1) * LANES])
        hi = _bf16_bits(t_ref[:, (2 * s + 1) * LANES:(2 * s + 2) * LANES])
        o_ref[pl.ds(s, rows, stride=ROW_WORDS), :] = lax.bitcast_convert_type(hi | (lo >> 16), I32)


def _pack_table(tab):
    n, d = tab.shape
    assert d == CHUNKS * LANES
    return pl.pallas_call(
        functools.partial(_pack_body, rows=PACK_TILE),
        grid=(n // PACK_TILE,),
        in_specs=[pl.BlockSpec((PACK_TILE, d), lambda i: (i, 0))],
        out_specs=pl.BlockSpec((PACK_TILE * ROW_WORDS, LANES), lambda i: (i, 0)),
        out_shape=jax.ShapeDtypeStruct((n * ROW_WORDS, LANES), I32),
        compiler_params=_params("parallel"),
    )(tab)


def _stage_rows(eid_ref, t, tab_ref, gbuf):
    for r in range(N_SLOTS):
        off = pl.multiple_of(eid_ref[t, r], ROW_WORDS)
        gbuf[r * ROW_WORDS:(r + 1) * ROW_WORDS, :] = tab_ref[pl.ds(off, ROW_WORDS), :]


def _staged(gbuf):
    return pltpu.bitcast(gbuf[...], BF16)


def _for_each_token(tb, eid_ref, tab_ref, gbufs, start, finish):
    _stage_rows(eid_ref, 0, tab_ref, gbufs[0])

    def group(i, carry):
        t0 = TOKENS_PER_ITER * i
        for j in range(TOKENS_PER_ITER):
            y = start(t0 + j, _staged(gbufs[j % 2]))
            _stage_rows(eid_ref, jnp.minimum(t0 + j + 1, tb - 1), tab_ref, gbufs[(j + 1) % 2])
            finish(t0 + j, y)
        return carry

    lax.fori_loop(0, tb // TOKENS_PER_ITER, group, 0)


def _chunk_mask():
    k = lax.broadcasted_iota(I32, (CHUNKS, GB_ROWS), 1)
    q = lax.broadcasted_iota(I32, (CHUNKS, GB_ROWS), 0)
    return (k & (CHUNKS - 1)) == q


def _token_rows(t):
    return pl.ds(pl.multiple_of(t * CHUNKS, CHUNKS), CHUNKS)


def _upass_body(eid_ref, xn_ref, gate_ref, tab_ref, rexp_ref, rsum_ref, w_ref, *scratch, tb):
    gbufs, (xbuf, zbuf) = scratch[:N_GBUF], scratch[N_GBUF:]
    for q in range(CHUNKS):
        xbuf[pl.ds(q, tb, stride=CHUNKS), :] = xn_ref[:, q * LANES:(q + 1) * LANES]
    mask = _chunk_mask()

    def start(t, g):
        x8 = xbuf[_token_rows(t), :].astype(BF16)
        return lax.dot_general(x8, g, _NT, preferred_element_type=F32)

    def finish(t, y):
        zbuf[pl.ds(t, 1), :] = jnp.sum(jnp.where(mask, y, 0.0), axis=0, keepdims=True)

    _for_each_token(tb, eid_ref, tab_ref, gbufs, start, finish)
    a = jnp.dot(zbuf[...], rsum_ref[...], preferred_element_type=F32, precision=lax.Precision.HIGHEST)
    act = 0.5 * a * (1.0 + lax.erf(a * math.sqrt(0.5)))
    w = (gate_ref[...] * act).astype(BF16)
    w_ref[...] = jnp.dot(w, rexp_ref[...], preferred_element_type=F32)


def _table_spec(tab):
    return pl.BlockSpec(tab.shape, lambda i: (0, 0), pipeline_mode=pl.Buffered(1))


def _gather_specs(d):
    tb = GATHER_TILE
    smem = pl.BlockSpec((tb, N_SLOTS), lambda i: (i, 0), memory_space=pltpu.SMEM)
    return tb, smem, (lambda c: pl.BlockSpec((tb, c), lambda i: (i, 0)))


def _upass(eid, xn, gate, tab, rexp, rsum):
    n, d = xn.shape
    tb, smem, row = _gather_specs(d)
    return pl.pallas_call(
        functools.partial(_upass_body, tb=tb),
        grid=(n // tb,),
        in_specs=[smem, row(d), row(N_SLOTS), _table_spec(tab), _resident(rexp.shape), _resident(rsum.shape)],
        out_specs=row(GB_ROWS),
        out_shape=jax.ShapeDtypeStruct((n, GB_ROWS), F32),
        scratch_shapes=_GBUFS + [
                        pltpu.VMEM((tb * CHUNKS, LANES), F32),
                        pltpu.VMEM((tb, GB_ROWS), F32)],
        compiler_params=_params("parallel"),
    )(eid, xn, gate, tab, rexp, rsum)


def _vpass_body(eid_ref, w_ref, h_ref, g_ref, tab_ref, y_ref, *scratch, tb):
    gbufs, pbuf = scratch[:N_GBUF], scratch[N_GBUF]
    mask = _chunk_mask()

    def start(t, g):
        wsel = jnp.where(mask, w_ref[pl.ds(t, 1), :], 0.0).astype(BF16)
        return jnp.dot(wsel, g, preferred_element_type=F32)

    def finish(t, y):
        pbuf[_token_rows(t), :] = y

    _for_each_token(tb, eid_ref, tab_ref, gbufs, start, finish)
    peer = jnp.concatenate([pbuf[pl.ds(q, tb, stride=CHUNKS), :] for q in range(CHUNKS)], axis=1)
    y_ref[...] = _rms(h_ref[...] + peer, g_ref[...])


def _vpass(eid, w, h, gfin, tab):
    n, d = h.shape
    tb, smem, row = _gather_specs(d)
    return pl.pallas_call(
        functools.partial(_vpass_body, tb=tb),
        grid=(n // tb,),
        in_specs=[smem, row(GB_ROWS), row(d), _resident((1, d)), _table_spec(tab)],
        out_specs=row(d),
        out_shape=jax.ShapeDtypeStruct((n, d), F32),
        scratch_shapes=_GBUFS + [pltpu.VMEM((tb * CHUNKS, LANES), F32)],
        compiler_params=_params("parallel"),
    )(eid, w, h, gfin, tab)


def _peer_and_final_norm(h, g_ffn, wq_bf, sk_bf, u_tab, v_tab, g_final):
    xn, eid, gate = _route(h, g_ffn, wq_bf, sk_bf)
    slot_of_row = jnp.arange(GB_ROWS, dtype=I32) // CHUNKS
    rexp = (jnp.arange(N_SLOTS, dtype=I32)[:, None] == slot_of_row[None, :])
    w = _upass(eid, xn, gate, u_tab, rexp.astype(BF16), rexp.T.astype(F32))
    return _vpass(eid, w, h, g_final.reshape(1, -1), v_tab)


def kernel(x_prompt, x_sample, cache_k, cache_v, state_pool, page_table, norm_mix_g, w_in, pool_group_w,
           pool_scale, lambda_q1, lambda_k1, lambda_q2, lambda_k2, subln_g, rel_bias, w_branch_pool,
           w_branch_attn, w_out, norm_ffn_g, peer_wq, peer_subkeys, peer_u, peer_v, norm_final_g):
    depth = w_in.shape[0]
    assert depth == 1
    b_p, s_p, d = x_prompt.shape
    b_s, t_s, _ = x_sample.shape
    psz = cache_k.shape[2]
    past = page_table.shape[1] * psz
    width = N_HEADS * V_DIM
    assert ATT_BLOCK >= MAX_DISTANCE and s_p % ATT_BLOCK == 0
    l = 0
    lam_init = 0.8 - 0.6 * math.exp(-0.3 * l)

    w_in_bf = w_in[l].astype(BF16)
    gw_bf = pool_group_w[l].astype(BF16)
    wbp, wba, wout = (w[l].astype(BF16) for w in (w_branch_pool, w_branch_attn, w_out))
    wq_bf = peer_wq[l].astype(BF16)
    sk_bf = peer_subkeys[l].astype(BF16).reshape(2 * PEER_HEADS, N_KEYS, PEER_HALF)
    u_tab = _pack_table(peer_u.reshape(-1, d))
    v_tab = _pack_table(peer_v.reshape(-1, d))
    g_mix = norm_mix_g[l].reshape(1, d)
    g_ffn = norm_ffn_g[l].reshape(1, d)
    scale = pool_scale[l].reshape(1, -1)
    sg = subln_g[l].reshape(1, V_DIM)
    lams = tuple(a[l].astype(F32).reshape(1, HEAD_DIM) for a in (lambda_q1, lambda_k1, lambda_q2, lambda_k2))

    r = jnp.arange(ATT_BLOCK, dtype=I32)
    dist = jnp.stack([dj * ATT_BLOCK + r[:, None] - r[None, :] for dj in range(3)])
    bias_tiles = _rel_bias(rel_bias, dist)

    tq = jnp.arange(t_s, dtype=I32)
    dist_past = past + tq[:, None] - jnp.arange(past, dtype=I32)[None, :]
    rows = 2 * N_HEADS * t_s
    per_map = lambda b: jnp.broadcast_to(b[:, None], (N_HEADS, 2) + b.shape[1:]).reshape(rows, -1)
    step_keys = PAGES_PER_STEP * psz
    bias_past = per_map(_rel_bias(rel_bias, dist_past))
    bias_past = bias_past.reshape(rows, past // step_keys, step_keys).transpose(1, 0, 2)
    dist_new = tq[:, None] - jnp.arange(psz, dtype=I32)[None, :]
    dist_new = jnp.where(jnp.arange(psz)[None, :] < t_s, dist_new, -1)
    bias_new = per_map(_rel_bias(rel_bias, dist_new))

    def token_stages(x, hist, pos0, attention):
        b, length, _ = x.shape
        n = b * length
        u, q, k, v, gp, ga = _inproj(x.reshape(n, d), g_mix, w_in_bf)
        pool_o = _pool(hist, u.reshape(b, length, -1), gw_bf, scale, pos0)
        attn_o = attention(q, k, v)
        h = _merge(x.reshape(n, d), pool_o.reshape(n, -1), attn_o, gp, ga, wbp, wba, wout)
        y = _peer_and_final_norm(h, g_ffn, wq_bf, sk_bf, u_tab, v_tab, norm_final_g)
        return y.reshape(b, length, d), u, k, v

    hist_p = jnp.zeros((b_p, HIST_ROWS, width), F32)
    attn_p = lambda q, k, v: _prompt_attention(q, k, v, bias_tiles, lams, sg, b_p, s_p, lam_init)
    y_p, u_p, k_p, v_p = token_stages(x_prompt, hist_p, 0, attn_p)

    hist_s = jnp.pad(state_pool[l], ((0, 0), (HIST_ROWS - POOL_STATE, 0), (0, 0)))
    ck = cache_k.reshape(-1, psz * N_HEADS, V_DIM)
    cv = cache_v.reshape(-1, psz * N_HEADS, V_DIM)

    def attn_s(q, k, v):
        q4 = q.reshape(b_s, t_s, N_HEADS, V_DIM).transpose(0, 2, 1, 3)
        first = jnp.arange(V_DIM) < HEAD_DIM
        zero = jnp.zeros((), BF16)
        q_heads = jnp.concatenate([jnp.where(first, q4, zero), jnp.where(first, zero, q4)], axis=2)
        pad = ((0, 0), (0, (psz - t_s) * N_HEADS), (0, 0))
        k_new = jnp.pad(k.reshape(b_s, t_s * N_HEADS, V_DIM), pad)
        v_new = jnp.pad(v.reshape(b_s, t_s * N_HEADS, V_DIM), pad)
        o = _sample_attention(page_table, q_heads, bias_past, bias_new, k_new, v_new, ck, cv, lams, sg,
                              t_s, lam_init)
        return o.reshape(b_s * t_s, width)

    y_s, u_s, k_s, v_s = token_stages(x_sample, hist_s, past, attn_s)

    kv_shape = lambda b, length: (1, b, length, N_HEADS, V_DIM)
    u_p3 = u_p.reshape(b_p, s_p, width)
    u_s3 = u_s.reshape(b_s, t_s, width)
    pool_prompt = u_p3[:, -POOL_STATE:][None]
    pool_sample = jnp.concatenate([state_pool[l], u_s3], axis=1)[:, -POOL_STATE:][None]
    return (y_p, y_s, k_p.reshape(kv_shape(b_p, s_p)), v_p.reshape(kv_shape(b_p, s_p)), pool_prompt,
            k_s.reshape(kv_shape(b_s, t_s)), v_s.reshape(kv_shape(b_s, t_s)), pool_sample)
```

```python
import functools
import math

import jax
import jax.numpy as jnp
from jax import lax
from jax.experimental import pallas as pl
from jax.experimental.pallas import tpu as pltpu

F32 = jnp.float32
BF16 = jnp.bfloat16
I32 = jnp.int32

EPS = 1e-6
NEG_INF = -1e30

LANES = 128
SUBLANES = 8
VMEM_LIMIT_BYTES = 56 * 1024 * 1024

POOL_WINDOWS = (2, 4, 8, 16)
POOL_STATE = max(POOL_WINDOWS) - 1
HIST_ROWS = 16
N_HEADS = 4
HEAD_DIM = 64
V_DIM = 2 * HEAD_DIM
N_BUCKETS = 32
MAX_DISTANCE = 128
PEER_HEADS = 8
PEER_TOPK = 16
N_KEYS = 128
PEER_HALF = 128
N_SLOTS = PEER_HEADS * PEER_TOPK
ROW_WORDS = 4
KEY_SHIFT = 14

TOK_TILE = 256
ATT_BLOCK = 512
PAGES_PER_STEP = 32
GATHER_TILE = 128
LISTS_PER_ITER = 4

_NT = (((1,), (1,)), ((), ()))


def _params(*sem):
    return pltpu.CompilerParams(dimension_semantics=sem, vmem_limit_bytes=VMEM_LIMIT_BYTES)


def _rms(x, g):
    return (x * lax.rsqrt(jnp.mean(x * x, axis=-1, keepdims=True) + EPS)) * g


def _resident(shape):
    nd = len(shape)
    return pl.BlockSpec(shape, lambda *_: (0,) * nd)


def _inproj_body(x_ref, g_ref, w_ref, u_ref, q_ref, k_ref, v_ref, gp_ref, ga_ref, *, splits):
    xb = _rms(x_ref[...], g_ref[...]).astype(BF16)

    def proj(i):
        return jnp.dot(xb, w_ref[:, splits[i]:splits[i + 1]], preferred_element_type=F32)

    u_ref[...] = proj(0)
    q_ref[...] = (proj(1) * (HEAD_DIM ** -0.5)).astype(BF16)
    k_ref[...] = proj(2)
    v_ref[...] = proj(3)
    gp_ref[...] = jax.nn.sigmoid(proj(4))
    ga_ref[...] = jax.nn.sigmoid(proj(5))


def _inproj(x, g, w_in_bf):
    n, d = x.shape
    pw = 512
    splits = (0, pw, 2 * pw, 3 * pw, 4 * pw, 4 * pw + d, 4 * pw + 2 * d)
    tm = TOK_TILE
    row = lambda c: pl.BlockSpec((tm, c), lambda i: (i, 0))
    return pl.pallas_call(
        functools.partial(_inproj_body, splits=splits),
        grid=(n // tm,),
        in_specs=[row(d), _resident((1, d)), _resident(w_in_bf.shape)],
        out_specs=[row(pw), row(pw), row(pw), row(pw), row(d), row(d)],
        out_shape=[jax.ShapeDtypeStruct((n, pw), F32), jax.ShapeDtypeStruct((n, pw), BF16),
                   jax.ShapeDtypeStruct((n, pw), F32), jax.ShapeDtypeStruct((n, pw), F32),
                   jax.ShapeDtypeStruct((n, d), F32), jax.ShapeDtypeStruct((n, d), F32)],
        compiler_params=_params("parallel"),
    )(x, g, w_in_bf)


def _pool_body(hist_ref, u_ref, gw_ref, sc_ref, o_ref, buf, *, length, pos0):
    buf[0:HIST_ROWS, :] = hist_ref[...]
    buf[HIST_ROWS:HIST_ROWS + length, :] = u_ref[...]
    t = lax.broadcasted_iota(I32, (length, 1), 0)
    for gi, w in enumerate(POOL_WINDOWS):
        c0 = gi * LANES
        ug = buf[HIST_ROWS:HIST_ROWS + length, c0:c0 + LANES]
        acc = ug
        for j in range(1, w):
            acc = acc + buf[HIST_ROWS - j:HIST_ROWS - j + length, c0:c0 + LANES]
        cnt = jnp.minimum(t + (pos0 + 1), w).astype(F32)
        mixed = acc / cnt - ug
        y = jnp.dot(mixed.astype(BF16), gw_ref[gi], preferred_element_type=F32)
        o_ref[:, c0:c0 + LANES] = y * sc_ref[:, c0:c0 + LANES]


def _pool(hist, u, gw_bf, scale, pos0):
    b, length, pw = u.shape
    return pl.pallas_call(
        functools.partial(_pool_body, length=length, pos0=pos0),
        grid=(b,),
        in_specs=[pl.BlockSpec((None, HIST_ROWS, pw), lambda i: (i, 0, 0)),
                  pl.BlockSpec((None, length, pw), lambda i: (i, 0, 0)),
                  _resident(gw_bf.shape), _resident((1, pw))],
        out_specs=pl.BlockSpec((None, length, pw), lambda i: (i, 0, 0)),
        out_shape=jax.ShapeDtypeStruct((b, length, pw), F32),
        scratch_shapes=[pltpu.VMEM((HIST_ROWS + length, pw), F32)],
        compiler_params=_params("parallel"),
    )(hist, u, gw_bf, scale)


def _lambda(lq1, lk1, lq2, lk2, lam_init):
    d1 = jnp.sum(lq1[...] * lk1[...], axis=-1, keepdims=True)
    d2 = jnp.sum(lq2[...] * lk2[...], axis=-1, keepdims=True)
    return jnp.exp(d1) - jnp.exp(d2) + lam_init


def _softmax_update(s, vals, m, l, acc):
    mn = jnp.maximum(m, jnp.max(s, axis=-1, keepdims=True))
    alpha = jnp.exp(m - mn)
    p = jnp.exp(s - mn)
    l = alpha * l + jnp.sum(p, axis=-1, keepdims=True)
    pb = p.astype(BF16)
    pv = None
    for (c0, c1), vb in vals:
        term = jnp.dot(pb[:, c0:c1], vb, preferred_element_type=F32)
        pv = term if pv is None else pv + term
    return mn, l, alpha * acc + pv


def _head_norm(o, sg, lam_init):
    return _rms(o, sg) * (1.0 - lam_init)


def _t5_bucket(n):
    max_exact = N_BUCKETS // 2
    nf = jnp.maximum(n, 1).astype(F32)
    large = max_exact + (jnp.log(nf / max_exact) / math.log(MAX_DISTANCE / max_exact)
                         * (N_BUCKETS - max_exact)).astype(I32)
    large = jnp.minimum(large, N_BUCKETS - 1)
    return jnp.where(n < max_exact, n, large)


def _rel_bias(rel_bias, n):
    bucket = _t5_bucket(jnp.maximum(n, 0))
    table = rel_bias.astype(F32)
    b = jnp.zeros((table.shape[1],) + n.shape, F32)
    for i in range(N_BUCKETS):
        b = jnp.where(bucket == i, table[i].reshape((-1,) + (1,) * n.ndim), b)
    return jnp.where(n >= 0, b, NEG_INF)


def _pattn_body(q_ref, k_ref, v_ref, bias_ref, lq1, lk1, lq2, lk2, sg_ref, o_ref, kb, vb,
                *, blk, lam_init):
    i = pl.program_id(2)

    @pl.when(i == 0)
    def _():
        kb[...] = k_ref[...].astype(BF16)
        vb[...] = v_ref[...].astype(BF16)

    q = q_ref[...]
    lane = lax.broadcasted_iota(I32, q.shape, 1)
    zero = jnp.zeros_like(q)
    q1 = jnp.where(lane < HEAD_DIM, q, zero)
    q2 = jnp.where(lane < HEAD_DIM, zero, q)

    def step(j, carry):
        m1, l1, a1, m2, l2, a2 = carry
        off = pl.multiple_of(j * blk, blk)
        kc = kb[pl.ds(off, blk), :]
        vc = vb[pl.ds(off, blk), :]
        bt = bias_ref[jnp.minimum(i - j, 2)]
        s1 = lax.dot_general(q1, kc, _NT, preferred_element_type=F32) + bt
        s2 = lax.dot_general(q2, kc, _NT, preferred_element_type=F32) + bt
        vals = [((0, blk), vc)]
        m1, l1, a1 = _softmax_update(s1, vals, m1, l1, a1)
        m2, l2, a2 = _softmax_update(s2, vals, m2, l2, a2)
        return m1, l1, a1, m2, l2, a2

    col = lambda v: jnp.full((blk, 1), v, F32)
    acc0 = jnp.zeros((blk, V_DIM), F32)
    init = (col(-jnp.inf), col(0.0), acc0, col(-jnp.inf), col(0.0), acc0)
    _, l1, a1, _, l2, a2 = lax.fori_loop(0, i + 1, step, init)
    lam = _lambda(lq1, lk1, lq2, lk2, lam_init)
    o = a1 / l1 - lam * (a2 / l2)
    o_ref[...] = _head_norm(o, sg_ref[...], lam_init)


def _prompt_attention(q, k, v, bias_tiles, lams, sg, batch, seq, lam_init):
    blk = ATT_BLOCK
    nq = seq // blk
    hw = V_DIM
    lam_spec = _resident((1, HEAD_DIM))
    return pl.pallas_call(
        functools.partial(_pattn_body, blk=blk, lam_init=lam_init),
        grid=(batch, N_HEADS, nq),
        in_specs=[pl.BlockSpec((blk, hw), lambda b, h, i: (b * nq + i, h)),
                  pl.BlockSpec((seq, hw), lambda b, h, i: (b, h)),
                  pl.BlockSpec((seq, hw), lambda b, h, i: (b, h)),
                  pl.BlockSpec((None, 3, blk, blk), lambda b, h, i: (h, 0, 0, 0)),
                  lam_spec, lam_spec, lam_spec, lam_spec, _resident((1, hw))],
        out_specs=pl.BlockSpec((blk, hw), lambda b, h, i: (b * nq + i, h)),
        out_shape=jax.ShapeDtypeStruct((batch * seq, N_HEADS * hw), F32),
        scratch_shapes=[pltpu.VMEM((seq, hw), BF16), pltpu.VMEM((seq, hw), BF16)],
        compiler_params=_params("parallel", "parallel", "arbitrary"),
    )(q, k, v, bias_tiles, *lams, sg)


def _sattn_body(pt_ref, q_ref, bias_ref, bnew_ref, knew_ref, vnew_ref, lq1, lk1, lq2, lk2, sg_ref,
                *rest, pages, t_new, psz, lam_init):
    k_refs = rest[:pages]
    v_refs = rest[pages:2 * pages]
    o_ref, m_s, l_s, acc_s = rest[2 * pages:]
    j = pl.program_id(1)
    hr = 2 * t_new

    @pl.when(j == 0)
    def _():
        m_s[...] = jnp.full(m_s.shape, -jnp.inf, F32)
        l_s[...] = jnp.zeros(l_s.shape, F32)
        acc_s[...] = jnp.zeros(acc_s.shape, F32)

    def head_rows(ref, h):
        return ref[pl.ds(h, psz, stride=N_HEADS), :].astype(BF16)

    def update(k_pages, v_pages, bias):
        s = jnp.concatenate(
            [jnp.concatenate([lax.dot_general(q_ref[h], head_rows(kp, h), _NT, preferred_element_type=F32)
                              for kp in k_pages], axis=1) for h in range(N_HEADS)], axis=0) + bias
        m = m_s[...]
        mn = jnp.maximum(m, jnp.max(s, axis=-1, keepdims=True))
        alpha = jnp.exp(m - mn)
        p = jnp.exp(s - mn)
        m_s[...] = mn
        l_s[...] = alpha * l_s[...] + jnp.sum(p, axis=-1, keepdims=True)
        pv = []
        for h in range(N_HEADS):
            ph = p[h * hr:(h + 1) * hr]
            terms = [jnp.dot(ph[:, i * psz:(i + 1) * psz].astype(BF16), head_rows(vp, h),
                             preferred_element_type=F32) for i, vp in enumerate(v_pages)]
            pv.append(functools.reduce(lambda a, b: a + b, terms))
        acc_s[...] = alpha * acc_s[...] + jnp.concatenate(pv, axis=0)

    update(k_refs, v_refs, bias_ref[j])

    @pl.when(j == pl.num_programs(1) - 1)
    def _():
        update([knew_ref], [vnew_ref], bnew_ref[...])
        lam = _lambda(lq1, lk1, lq2, lk2, lam_init)
        o = acc_s[...] / l_s[...]
        for h in range(N_HEADS):
            r0 = h * hr
            od = o[r0:r0 + t_new] - lam * o[r0 + t_new:r0 + hr]
            o_ref[:, h * V_DIM:(h + 1) * V_DIM] = _head_norm(od, sg_ref[...], lam_init)


def _sample_attention(page_table, q_heads, bias_past, bias_new, k_new, v_new, cache_k, cache_v, lams, sg,
                      t_new, lam_init):
    b, n_pages = page_table.shape
    pages = PAGES_PER_STEP
    nj = n_pages // pages
    page_rows = cache_k.shape[1]
    psz = page_rows // N_HEADS
    rows = 2 * N_HEADS * t_new
    lam_spec = _resident((1, HEAD_DIM))

    def page_spec(p):
        return pl.BlockSpec((None, page_rows, V_DIM), lambda i, j, pt: (pt[i, j * pages + p], 0, 0))

    new_spec = pl.BlockSpec((None, page_rows, V_DIM), lambda i, j, pt: (i, 0, 0))
    grid_spec = pltpu.PrefetchScalarGridSpec(
        num_scalar_prefetch=1,
        grid=(b, nj),
        in_specs=[pl.BlockSpec((None, N_HEADS, 2 * t_new, V_DIM), lambda i, j, pt: (i, 0, 0, 0)),
                  _resident(bias_past.shape), _resident(bias_new.shape), new_spec, new_spec,
                  lam_spec, lam_spec, lam_spec, lam_spec, _resident((1, V_DIM))]
                 + [page_spec(p) for p in range(pages)] + [page_spec(p) for p in range(pages)],
        out_specs=pl.BlockSpec((None, t_new, N_HEADS * V_DIM), lambda i, j, pt: (i, 0, 0)),
        scratch_shapes=[pltpu.VMEM((rows, 1), F32), pltpu.VMEM((rows, 1), F32),
                        pltpu.VMEM((rows, V_DIM), F32)],
    )
    return pl.pallas_call(
        functools.partial(_sattn_body, pages=pages, t_new=t_new, psz=psz, lam_init=lam_init),
        grid_spec=grid_spec,
        out_shape=jax.ShapeDtypeStruct((b, t_new, N_HEADS * V_DIM), F32),
        compiler_params=_params("parallel", "arbitrary"),
    )(page_table, q_heads, bias_past, bias_new, k_new, v_new, *lams, sg,
      *([cache_k] * pages), *([cache_v] * pages))


def _merge_body(x_ref, po_ref, ao_ref, gp_ref, ga_ref, wbp_ref, wba_ref, wout_ref, h_ref):
    a = jnp.dot(po_ref[...].astype(BF16), wbp_ref[...], preferred_element_type=F32)
    b = jnp.dot(ao_ref[...].astype(BF16), wba_ref[...], preferred_element_type=F32)
    m = gp_ref[...] * a + ga_ref[...] * b
    h_ref[...] = x_ref[...] + jnp.dot(m.astype(BF16), wout_ref[...], preferred_element_type=F32)


def _merge(x, pool_o, attn_o, gp, ga, wbp, wba, wout):
    n, d = x.shape
    tm = TOK_TILE
    row = lambda c: pl.BlockSpec((tm, c), lambda i: (i, 0))
    return pl.pallas_call(
        _merge_body,
        grid=(n // tm,),
        in_specs=[row(d), row(pool_o.shape[1]), row(attn_o.shape[1]), row(d), row(d),
                  _resident(wbp.shape), _resident(wba.shape), _resident(wout.shape)],
        out_specs=row(d),
        out_shape=jax.ShapeDtypeStruct((n, d), F32),
        compiler_params=_params("parallel"),
    )(x, pool_o, attn_o, gp, ga, wbp, wba, wout)


def _extract_topk(x, key, k):
    big = jnp.iinfo(jnp.int32).max
    vals, keys = [], []
    for _ in range(k):
        m = jnp.max(x, axis=0, keepdims=True)
        kmin = jnp.min(jnp.where(x == m, key, big), axis=0, keepdims=True)
        x = jnp.where(key == kmin, -jnp.inf, x)
        vals.append(m)
        keys.append(kmin)
    return jnp.concatenate(vals, axis=0), jnp.concatenate(keys, axis=0)


def _route_body(h_ref, g_ref, wq_ref, sk_ref, xn_ref, eid_ref, gate_ref, q_scr, sv_scr, si_scr,
                eid_scr, gate_scr, *, tm):
    xn = _rms(h_ref[...], g_ref[...])
    xn_ref[...] = xn
    q = jnp.dot(xn.astype(BF16), wq_ref[...], preferred_element_type=F32)
    n_lists = 2 * PEER_HEADS
    for l in range(n_lists):
        q_scr[l] = q[:, l * PEER_HALF:(l + 1) * PEER_HALF].astype(BF16)
    groups = tm // LANES
    key_iota = lax.broadcasted_iota(I32, (N_KEYS, LANES), 0)

    def list_body(i, carry):
        for l in [LISTS_PER_ITER * i + k for k in range(LISTS_PER_ITER)]:
            st = lax.dot_general(sk_ref[l], q_scr[l], _NT, preferred_element_type=F32)
            for g in range(groups):
                cs = slice(g * LANES, (g + 1) * LANES)
                vals, idx = _extract_topk(st[:, cs], key_iota, PEER_TOPK)
                sv_scr[l, :, cs] = vals
                si_scr[l, :, cs] = idx
        return carry

    lax.fori_loop(0, n_lists // LISTS_PER_ITER, list_body, 0)

    j_iota = lax.broadcasted_iota(I32, (PEER_TOPK, LANES), 0)

    def head_body(hh, carry):
        rows = pl.ds(pl.multiple_of(hh * PEER_TOPK, PEER_TOPK), PEER_TOPK)
        for g in range(groups):
            cs = slice(g * LANES, (g + 1) * LANES)
            a = sv_scr[2 * hh, :, cs]
            b = sv_scr[2 * hh + 1, :, cs]
            ia = si_scr[2 * hh, :, cs]
            ib = si_scr[2 * hh + 1, :, cs]
            kb = (j_iota << KEY_SHIFT) + ib
            row_key = lambda i: ia[i:i + 1] * N_KEYS + ((i * PEER_TOPK) << KEY_SHIFT)
            half = PEER_TOPK // 2
            cands = [a[0:1] + b]
            keys = [kb + row_key(0)]
            for i in range(1, half):
                valid = j_iota[:half] < PEER_TOPK // (i + 1)
                cands.append(jnp.where(valid, a[i:i + 1] + b[:half], -jnp.inf))
                keys.append(kb[:half] + row_key(i))
            cands.append(a[half:] + b[0:1])
            keys.append(((j_iota[half:] * PEER_TOPK) << KEY_SHIFT) + ia[half:] * N_KEYS + ib[0:1])
            top_s, top_key = _extract_topk(jnp.concatenate(cands, axis=0), jnp.concatenate(keys, axis=0),
                                           PEER_TOPK)
            e = jnp.exp(top_s - top_s[0:1])
            eid_scr[rows, cs] = (top_key & ((1 << KEY_SHIFT) - 1)) * ROW_WORDS
            gate_scr[rows, cs] = e / jnp.sum(e, axis=0, keepdims=True)
        return carry

    lax.fori_loop(0, PEER_HEADS, head_body, 0)
    eid_ref[...] = eid_scr[...].T
    gate_ref[...] = gate_scr[...].T


def _route(h, g, wq_bf, sk_bf):
    n, d = h.shape
    tm = TOK_TILE
    n_lists = 2 * PEER_HEADS
    row = lambda c: pl.BlockSpec((tm, c), lambda i: (i, 0))
    return pl.pallas_call(
        functools.partial(_route_body, tm=tm),
        grid=(n // tm,),
        in_specs=[row(d), _resident((1, d)), _resident(wq_bf.shape), _resident(sk_bf.shape)],
        out_specs=[row(d), row(N_SLOTS), row(N_SLOTS)],
        out_shape=[jax.ShapeDtypeStruct((n, d), F32), jax.ShapeDtypeStruct((n, N_SLOTS), I32),
                   jax.ShapeDtypeStruct((n, N_SLOTS), F32)],
        scratch_shapes=[pltpu.VMEM((n_lists, tm, PEER_HALF), BF16),
                        pltpu.VMEM((n_lists, PEER_TOPK, tm), F32),
                        pltpu.VMEM((n_lists, PEER_TOPK, tm), I32),
                        pltpu.VMEM((N_SLOTS, tm), I32),
                        pltpu.VMEM((N_SLOTS, tm), F32)],
        compiler_params=_params("parallel"),
    )(h, g, wq_bf, sk_bf)


CHUNKS = 2 * ROW_WORDS
GB_ROWS = N_SLOTS * CHUNKS
N_GBUF = 2
TOKENS_PER_ITER = 16
_GBUFS = [pltpu.VMEM((N_SLOTS * ROW_WORDS, LANES), I32)] * N_GBUF


PACK_TILE = 512


def _bf16_bits(x):
    b = lax.bitcast_convert_type(x, jnp.uint32)
    return (b + 0x7FFF + ((b >> 16) & 1)) & jnp.uint32(0xFFFF0000)


def _pack_body(t_ref, o_ref, *, rows):
    for s in range(ROW_WORDS):
        lo = _bf16_bits(t_ref[:, (2 * s) * LANES:(2 * s + 1) * LANES])
        hi = _bf16_bits(t_ref[:, (2 * s + 1) * LANES:(2 * s + 2) * LANES])
        o_ref[pl.ds(s, rows, stride=ROW_WORDS), :] = lax.bitcast_convert_type(hi | (lo >> 16), I32)


def _pack_table(tab):
    n, d = tab.shape
    assert d == CHUNKS * LANES
    return pl.pallas_call(
        functools.partial(_pack_body, rows=PACK_TILE),
        grid=(n // PACK_TILE,),
        in_specs=[pl.BlockSpec((PACK_TILE, d), lambda i: (i, 0))],
        out_specs=pl.BlockSpec((PACK_TILE * ROW_WORDS, LANES), lambda i: (i, 0)),
        out_shape=jax.ShapeDtypeStruct((n * ROW_WORDS, LANES), I32),
        compiler_params=_params("parallel"),
    )(tab)


def _stage_rows(eid_ref, t, tab_ref, gbuf):
    for r in range(N_SLOTS):
        off = pl.multiple_of(eid_ref[t, r], ROW_WORDS)
        gbuf[r * ROW_WORDS:(r + 1) * ROW_WORDS, :] = tab_ref[pl.ds(off, ROW_WORDS), :]


def _staged(gbuf):
    return pltpu.bitcast(gbuf[...], BF16)


def _for_each_token(tb, eid_ref, tab_ref, gbufs, start, finish):
    _stage_rows(eid_ref, 0, tab_ref, gbufs[0])

    def group(i, carry):
        t0 = TOKENS_PER_ITER * i
        for j in range(TOKENS_PER_ITER):
            y = start(t0 + j, _staged(gbufs[j % 2]))
            _stage_rows(eid_ref, jnp.minimum(t0 + j + 1, tb - 1), tab_ref, gbufs[(j + 1) % 2])
            finish(t0 + j, y)
        return carry

    lax.fori_loop(0, tb // TOKENS_PER_ITER, group, 0)


def _chunk_mask():
    k = lax.broadcasted_iota(I32, (CHUNKS, GB_ROWS), 1)
    q = lax.broadcasted_iota(I32, (CHUNKS, GB_ROWS), 0)
    return (k & (CHUNKS - 1)) == q


def _token_rows(t):
    return pl.ds(pl.multiple_of(t * CHUNKS, CHUNKS), CHUNKS)


def _upass_body(eid_ref, xn_ref, gate_ref, tab_ref, rexp_ref, rsum_ref, w_ref, *scratch, tb):
    gbufs, (xbuf, zbuf) = scratch[:N_GBUF], scratch[N_GBUF:]
    for q in range(CHUNKS):
        xbuf[pl.ds(q, tb, stride=CHUNKS), :] = xn_ref[:, q * LANES:(q + 1) * LANES]
    mask = _chunk_mask()

    def start(t, g):
        x8 = xbuf[_token_rows(t), :].astype(BF16)
        return lax.dot_general(x8, g, _NT, preferred_element_type=F32)

    def finish(t, y):
        zbuf[pl.ds(t, 1), :] = jnp.sum(jnp.where(mask, y, 0.0), axis=0, keepdims=True)

    _for_each_token(tb, eid_ref, tab_ref, gbufs, start, finish)
    a = jnp.dot(zbuf[...], rsum_ref[...], preferred_element_type=F32, precision=lax.Precision.HIGHEST)
    act = 0.5 * a * (1.0 + lax.erf(a * math.sqrt(0.5)))
    w = (gate_ref[...] * act).astype(BF16)
    w_ref[...] = jnp.dot(w, rexp_ref[...], preferred_element_type=F32)


def _table_spec(tab):
    return pl.BlockSpec(tab.shape, lambda i: (0, 0), pipeline_mode=pl.Buffered(1))


def _gather_specs(d):
    tb = GATHER_TILE
    smem = pl.BlockSpec((tb, N_SLOTS), lambda i: (i, 0), memory_space=pltpu.SMEM)
    return tb, smem, (lambda c: pl.BlockSpec((tb, c), lambda i: (i, 0)))


def _upass(eid, xn, gate, tab, rexp, rsum):
    n, d = xn.shape
    tb, smem, row = _gather_specs(d)
    return pl.pallas_call(
        functools.partial(_upass_body, tb=tb),
        grid=(n // tb,),
        in_specs=[smem, row(d), row(N_SLOTS), _table_spec(tab), _resident(rexp.shape), _resident(rsum.shape)],
        out_specs=row(GB_ROWS),
        out_shape=jax.ShapeDtypeStruct((n, GB_ROWS), F32),
        scratch_shapes=_GBUFS + [
                        pltpu.VMEM((tb * CHUNKS, LANES), F32),
                        pltpu.VMEM((tb, GB_ROWS), F32)],
        compiler_params=_params("parallel"),
    )(eid, xn, gate, tab, rexp, rsum)


def _vpass_body(eid_ref, w_ref, h_ref, g_ref, tab_ref, y_ref, *scratch, tb):
    gbufs, pbuf = scratch[:N_GBUF], scratch[N_GBUF]
    mask = _chunk_mask()

    def start(t, g):
        wsel = jnp.where(mask, w_ref[pl.ds(t, 1), :], 0.0).astype(BF16)
        return jnp.dot(wsel, g, preferred_element_type=F32)

    def finish(t, y):
        pbuf[_token_rows(t), :] = y

    _for_each_token(tb, eid_ref, tab_ref, gbufs, start, finish)
    peer = jnp.concatenate([pbuf[pl.ds(q, tb, stride=CHUNKS), :] for q in range(CHUNKS)], axis=1)
    y_ref[...] = _rms(h_ref[...] + peer, g_ref[...])


def _vpass(eid, w, h, gfin, tab):
    n, d = h.shape
    tb, smem, row = _gather_specs(d)
    return pl.pallas_call(
        functools.partial(_vpass_body, tb=tb),
        grid=(n // tb,),
        in_specs=[smem, row(GB_ROWS), row(d), _resident((1, d)), _table_spec(tab)],
        out_specs=row(d),
        out_shape=jax.ShapeDtypeStruct((n, d), F32),
        scratch_shapes=_GBUFS + [pltpu.VMEM((tb * CHUNKS, LANES), F32)],
        compiler_params=_params("parallel"),
    )(eid, w, h, gfin, tab)


def _peer_and_final_norm(h, g_ffn, wq_bf, sk_bf, u_tab, v_tab, g_final):
    xn, eid, gate = _route(h, g_ffn, wq_bf, sk_bf)
    slot_of_row = jnp.arange(GB_ROWS, dtype=I32) // CHUNKS
    rexp = (jnp.arange(N_SLOTS, dtype=I32)[:, None] == slot_of_row[None, :])
    w = _upass(eid, xn, gate, u_tab, rexp.astype(BF16), rexp.T.astype(F32))
    return _vpass(eid, w, h, g_final.reshape(1, -1), v_tab)


def kernel(x_prompt, x_sample, cache_k, cache_v, state_pool, page_table, norm_mix_g, w_in, pool_group_w,
           pool_scale, lambda_q1, lambda_k1, lambda_q2, lambda_k2, subln_g, rel_bias, w_branch_pool,
           w_branch_attn, w_out, norm_ffn_g, peer_wq, peer_subkeys, peer_u, peer_v, norm_final_g):
    depth = w_in.shape[0]
    assert depth == 1
    b_p, s_p, d = x_prompt.shape
    b_s, t_s, _ = x_sample.shape
    psz = cache_k.shape[2]
    past = page_table.shape[1] * psz
    width = N_HEADS * V_DIM
    assert ATT_BLOCK >= MAX_DISTANCE and s_p % ATT_BLOCK == 0
    l = 0
    lam_init = 0.8 - 0.6 * math.exp(-0.3 * l)

    w_in_bf = w_in[l].astype(BF16)
    gw_bf = pool_group_w[l].astype(BF16)
    wbp, wba, wout = (w[l].astype(BF16) for w in (w_branch_pool, w_branch_attn, w_out))
    wq_bf = peer_wq[l].astype(BF16)
    sk_bf = peer_subkeys[l].astype(BF16).reshape(2 * PEER_HEADS, N_KEYS, PEER_HALF)
    u_tab = _pack_table(peer_u.reshape(-1, d))
    v_tab = _pack_table(peer_v.reshape(-1, d))
    g_mix = norm_mix_g[l].reshape(1, d)
    g_ffn = norm_ffn_g[l].reshape(1, d)
    scale = pool_scale[l].reshape(1, -1)
    sg = subln_g[l].reshape(1, V_DIM)
    lams = tuple(a[l].astype(F32).reshape(1, HEAD_DIM) for a in (lambda_q1, lambda_k1, lambda_q2, lambda_k2))

    r = jnp.arange(ATT_BLOCK, dtype=I32)
    dist = jnp.stack([dj * ATT_BLOCK + r[:, None] - r[None, :] for dj in range(3)])
    bias_tiles = _rel_bias(rel_bias, dist)

    tq = jnp.arange(t_s, dtype=I32)
    dist_past = past + tq[:, None] - jnp.arange(past, dtype=I32)[None, :]
    rows = 2 * N_HEADS * t_s
    per_map = lambda b: jnp.broadcast_to(b[:, None], (N_HEADS, 2) + b.shape[1:]).reshape(rows, -1)
    step_keys = PAGES_PER_STEP * psz
    bias_past = per_map(_rel_bias(rel_bias, dist_past))
    bias_past = bias_past.reshape(rows, past // step_keys, step_keys).transpose(1, 0, 2)
    dist_new = tq[:, None] - jnp.arange(psz, dtype=I32)[None, :]
    dist_new = jnp.where(jnp.arange(psz)[None, :] < t_s, dist_new, -1)
    bias_new = per_map(_rel_bias(rel_bias, dist_new))

    def token_stages(x, hist, pos0, attention):
        b, length, _ = x.shape
        n = b * length
        u, q, k, v, gp, ga = _inproj(x.reshape(n, d), g_mix, w_in_bf)
        pool_o = _pool(hist, u.reshape(b, length, -1), gw_bf, scale, pos0)
        attn_o = attention(q, k, v)
        h = _merge(x.reshape(n, d), pool_o.reshape(n, -1), attn_o, gp, ga, wbp, wba, wout)
        y = _peer_and_final_norm(h, g_ffn, wq_bf, sk_bf, u_tab, v_tab, norm_final_g)
        return y.reshape(b, length, d), u, k, v

    hist_p = jnp.zeros((b_p, HIST_ROWS, width), F32)
    attn_p = lambda q, k, v: _prompt_attention(q, k, v, bias_tiles, lams, sg, b_p, s_p, lam_init)
    y_p, u_p, k_p, v_p = token_stages(x_prompt, hist_p, 0, attn_p)

    hist_s = jnp.pad(state_pool[l], ((0, 0), (HIST_ROWS - POOL_STATE, 0), (0, 0)))
    ck = cache_k.reshape(-1, psz * N_HEADS, V_DIM)
    cv = cache_v.reshape(-1, psz * N_HEADS, V_DIM)

    def attn_s(q, k, v):
        q4 = q.reshape(b_s, t_s, N_HEADS, V_DIM).transpose(0, 2, 1, 3)
        first = jnp.arange(V_DIM) < HEAD_DIM
        zero = jnp.zeros((), BF16)
        q_heads = jnp.concatenate([jnp.where(first, q4, zero), jnp.where(first, zero, q4)], axis=2)
        pad = ((0, 0), (0, (psz - t_s) * N_HEADS), (0, 0))
        k_new = jnp.pad(k.reshape(b_s, t_s * N_HEADS, V_DIM), pad)
        v_new = jnp.pad(v.reshape(b_s, t_s * N_HEADS, V_DIM), pad)
        o = _sample_attention(page_table, q_heads, bias_past, bias_new, k_new, v_new, ck, cv, lams, sg,
                              t_s, lam_init)
        return o.reshape(b_s * t_s, width)

    y_s, u_s, k_s, v_s = token_stages(x_sample, hist_s, past, attn_s)

    kv_shape = lambda b, length: (1, b, length, N_HEADS, V_DIM)
    u_p3 = u_p.reshape(b_p, s_p, width)
    u_s3 = u_s.reshape(b_s, t_s, width)
    pool_prompt = u_p3[:, -POOL_STATE:][None]
    pool_sample = jnp.concatenate([state_pool[l], u_s3], axis=1)[:, -POOL_STATE:][None]
    return (y_p, y_s, k_p.reshape(kv_shape(b_p, s_p)), v_p.reshape(kv_shape(b_p, s_p)), pool_prompt,
            k_s.reshape(kv_shape(b_s, t_s)), v_s.reshape(kv_shape(b_s, t_s)), pool_sample)
```

```python
import functools
import math

import jax
import jax.numpy as jnp
from jax import lax
from jax.experimental import pallas as pl
from jax.experimental.pallas import tpu as pltpu

F32 = jnp.float32
BF16 = jnp.bfloat16
I32 = jnp.int32

EPS = 1e-6
NEG_INF = -1e30

LANES = 128
SUBLANES = 8
VMEM_BYTES = 64 * 1024 * 1024
VMEM_LIMIT_BYTES = VMEM_BYTES - 8 * 1024 * 1024

POOL_WINDOWS = (2, 4, 8, 16)
POOL_STATE = max(POOL_WINDOWS) - 1
HIST_ROWS = -(-POOL_STATE // SUBLANES) * SUBLANES
N_HEADS = 4
HEAD_DIM = 64
V_DIM = 2 * HEAD_DIM
MIX_WIDTH = N_HEADS * V_DIM
N_BUCKETS = 32
MAX_DISTANCE = 128
PEER_HEADS = 8
PEER_TOPK = 16
N_KEYS = 128
PEER_HALF = 128
N_SLOTS = PEER_HEADS * PEER_TOPK
ROW_WORDS = 4
KEY_SHIFT = (N_KEYS * N_KEYS - 1).bit_length()

TOK_TILE = 256
ATT_BLOCK = 512
PAGES_PER_STEP = 32
GATHER_TILE = 128
LISTS_PER_ITER = 8

_NT = (((1,), (1,)), ((), ()))


def _params(*sem):
    return pltpu.CompilerParams(dimension_semantics=sem, vmem_limit_bytes=VMEM_LIMIT_BYTES)


def _rms(x, g):
    return (x * lax.rsqrt(jnp.mean(x * x, axis=-1, keepdims=True) + EPS)) * g


def _resident(shape):
    nd = len(shape)
    return pl.BlockSpec(shape, lambda *_: (0,) * nd)


def _inproj_body(x_ref, g_ref, w_ref, u_ref, q_ref, k_ref, v_ref, gp_ref, ga_ref, *, splits):
    xb = _rms(x_ref[...], g_ref[...]).astype(BF16)

    def proj(i):
        return jnp.dot(xb, w_ref[:, splits[i]:splits[i + 1]], preferred_element_type=F32)

    u_ref[...] = proj(0)
    q_ref[...] = (proj(1) * (HEAD_DIM ** -0.5)).astype(BF16)
    k_ref[...] = proj(2)
    v_ref[...] = proj(3)
    gp_ref[...] = jax.nn.sigmoid(proj(4))
    ga_ref[...] = jax.nn.sigmoid(proj(5))


def _inproj(x, g, w_in_bf):
    n, d = x.shape
    pw = MIX_WIDTH
    assert w_in_bf.shape == (d, 4 * pw + 2 * d)
    splits = (0, pw, 2 * pw, 3 * pw, 4 * pw, 4 * pw + d, 4 * pw + 2 * d)
    tm = TOK_TILE
    row = lambda c: pl.BlockSpec((tm, c), lambda i: (i, 0))
    return pl.pallas_call(
        functools.partial(_inproj_body, splits=splits),
        grid=(n // tm,),
        in_specs=[row(d), _resident((1, d)), _resident(w_in_bf.shape)],
        out_specs=[row(pw), row(pw), row(pw), row(pw), row(d), row(d)],
        out_shape=[jax.ShapeDtypeStruct((n, pw), F32), jax.ShapeDtypeStruct((n, pw), BF16),
                   jax.ShapeDtypeStruct((n, pw), F32), jax.ShapeDtypeStruct((n, pw), F32),
                   jax.ShapeDtypeStruct((n, d), F32), jax.ShapeDtypeStruct((n, d), F32)],
        compiler_params=_params("parallel"),
    )(x, g, w_in_bf)


def _pool_body(hist_ref, u_ref, gw_ref, sc_ref, o_ref, buf, *, length, pos0):
    buf[0:HIST_ROWS, :] = hist_ref[...]
    buf[HIST_ROWS:HIST_ROWS + length, :] = u_ref[...]
    t = lax.broadcasted_iota(I32, (length, 1), 0)
    for gi, w in enumerate(POOL_WINDOWS):
        c0 = gi * LANES
        ug = buf[HIST_ROWS:HIST_ROWS + length, c0:c0 + LANES]
        acc = ug
        for j in range(1, w):
            acc = acc + buf[HIST_ROWS - j:HIST_ROWS - j + length, c0:c0 + LANES]
        cnt = jnp.minimum(t + (pos0 + 1), w).astype(F32)
        mixed = acc / cnt - ug
        y = jnp.dot(mixed.astype(BF16), gw_ref[gi], preferred_element_type=F32)
        o_ref[:, c0:c0 + LANES] = y * sc_ref[:, c0:c0 + LANES]


def _pool(hist, u, gw_bf, scale, pos0):
    b, length, pw = u.shape
    return pl.pallas_call(
        functools.partial(_pool_body, length=length, pos0=pos0),
        grid=(b,),
        in_specs=[pl.BlockSpec((None, HIST_ROWS, pw), lambda i: (i, 0, 0)),
                  pl.BlockSpec((None, length, pw), lambda i: (i, 0, 0)),
                  _resident(gw_bf.shape), _resident((1, pw))],
        out_specs=pl.BlockSpec((None, length, pw), lambda i: (i, 0, 0)),
        out_shape=jax.ShapeDtypeStruct((b, length, pw), F32),
        scratch_shapes=[pltpu.VMEM((HIST_ROWS + length, pw), F32)],
        compiler_params=_params("parallel"),
    )(hist, u, gw_bf, scale)


def _lambda(lq1, lk1, lq2, lk2, lam_init):
    d1 = jnp.sum(lq1[...] * lk1[...], axis=-1, keepdims=True)
    d2 = jnp.sum(lq2[...] * lk2[...], axis=-1, keepdims=True)
    return jnp.exp(d1) - jnp.exp(d2) + lam_init


def _softmax_update(s, vals, m, l, acc):
    mn = jnp.maximum(m, jnp.max(s, axis=-1, keepdims=True))
    alpha = jnp.exp(m - mn)
    p = jnp.exp(s - mn)
    l = alpha * l + jnp.sum(p, axis=-1, keepdims=True)
    pb = p.astype(BF16)
    pv = None
    for (c0, c1), vb in vals:
        term = jnp.dot(pb[:, c0:c1], vb, preferred_element_type=F32)
        pv = term if pv is None else pv + term
    return mn, l, alpha * acc + pv


def _head_norm(o, sg, lam_init):
    return _rms(o, sg) * (1.0 - lam_init)


def _t5_bucket(n):
    max_exact = N_BUCKETS // 2
    nf = jnp.maximum(n, 1).astype(F32)
    large = max_exact + (jnp.log(nf / max_exact) / math.log(MAX_DISTANCE / max_exact)
                         * (N_BUCKETS - max_exact)).astype(I32)
    large = jnp.minimum(large, N_BUCKETS - 1)
    return jnp.where(n < max_exact, n, large)


def _rel_bias(rel_bias, n):
    bucket = _t5_bucket(jnp.maximum(n, 0))
    table = rel_bias.astype(F32)
    b = jnp.zeros((table.shape[1],) + n.shape, F32)
    for i in range(N_BUCKETS):
        b = jnp.where(bucket == i, table[i].reshape((-1,) + (1,) * n.ndim), b)
    return jnp.where(n >= 0, b, NEG_INF)


def _pattn_body(q_ref, k_ref, v_ref, bias_ref, lq1, lk1, lq2, lk2, sg_ref, o_ref, kb, vb,
                *, blk, lam_init):
    i = pl.program_id(2)

    @pl.when(i == 0)
    def _():
        kb[...] = k_ref[...].astype(BF16)
        vb[...] = v_ref[...].astype(BF16)

    q = q_ref[...]
    lane = lax.broadcasted_iota(I32, q.shape, 1)
    zero = jnp.zeros_like(q)
    q1 = jnp.where(lane < HEAD_DIM, q, zero)
    q2 = jnp.where(lane < HEAD_DIM, zero, q)

    def step(j, carry):
        m1, l1, a1, m2, l2, a2 = carry
        off = pl.multiple_of(j * blk, blk)
        kc = kb[pl.ds(off, blk), :]
        vc = vb[pl.ds(off, blk), :]
        bt = bias_ref[jnp.minimum(i - j, 2)]
        s1 = lax.dot_general(q1, kc, _NT, preferred_element_type=F32) + bt
        s2 = lax.dot_general(q2, kc, _NT, preferred_element_type=F32) + bt
        vals = [((0, blk), vc)]
        m1, l1, a1 = _softmax_update(s1, vals, m1, l1, a1)
        m2, l2, a2 = _softmax_update(s2, vals, m2, l2, a2)
        return m1, l1, a1, m2, l2, a2

    col = lambda v: jnp.full((blk, 1), v, F32)
    acc0 = jnp.zeros((blk, V_DIM), F32)
    init = (col(-jnp.inf), col(0.0), acc0, col(-jnp.inf), col(0.0), acc0)
    _, l1, a1, _, l2, a2 = lax.fori_loop(0, i + 1, step, init)
    lam = _lambda(lq1, lk1, lq2, lk2, lam_init)
    o = a1 / l1 - lam * (a2 / l2)
    o_ref[...] = _head_norm(o, sg_ref[...], lam_init)


def _prompt_attention(q, k, v, bias_tiles, lams, sg, batch, seq, lam_init):
    blk = ATT_BLOCK
    nq = seq // blk
    hw = V_DIM
    lam_spec = _resident((1, HEAD_DIM))
    return pl.pallas_call(
        functools.partial(_pattn_body, blk=blk, lam_init=lam_init),
        grid=(batch, N_HEADS, nq),
        in_specs=[pl.BlockSpec((blk, hw), lambda b, h, i: (b * nq + i, h)),
                  pl.BlockSpec((seq, hw), lambda b, h, i: (b, h)),
                  pl.BlockSpec((seq, hw), lambda b, h, i: (b, h)),
                  pl.BlockSpec((None, 3, blk, blk), lambda b, h, i: (h, 0, 0, 0)),
                  lam_spec, lam_spec, lam_spec, lam_spec, _resident((1, hw))],
        out_specs=pl.BlockSpec((blk, hw), lambda b, h, i: (b * nq + i, h)),
        out_shape=jax.ShapeDtypeStruct((batch * seq, N_HEADS * hw), F32),
        scratch_shapes=[pltpu.VMEM((seq, hw), BF16), pltpu.VMEM((seq, hw), BF16)],
        compiler_params=_params("parallel", "parallel", "arbitrary"),
    )(q, k, v, bias_tiles, *lams, sg)


def _sattn_body(pt_ref, q_ref, bias_ref, bnew_ref, knew_ref, vnew_ref, lq1, lk1, lq2, lk2, sg_ref,
                *rest, pages, t_new, psz, lam_init):
    k_refs = rest[:pages]
    v_refs = rest[pages:2 * pages]
    o_ref, m_s, l_s, acc_s = rest[2 * pages:]
    j = pl.program_id(1)
    hr = 2 * t_new

    @pl.when(j == 0)
    def _():
        m_s[...] = jnp.full(m_s.shape, -jnp.inf, F32)
        l_s[...] = jnp.zeros(l_s.shape, F32)
        acc_s[...] = jnp.zeros(acc_s.shape, F32)

    def head_rows(ref, h):
        return ref[pl.ds(h, psz, stride=N_HEADS), :].astype(BF16)

    def update(k_pages, v_pages, bias):
        s = jnp.concatenate(
            [jnp.concatenate([lax.dot_general(q_ref[h], head_rows(kp, h), _NT, preferred_element_type=F32)
                              for kp in k_pages], axis=1) for h in range(N_HEADS)], axis=0) + bias
        m = m_s[...]
        mn = jnp.maximum(m, jnp.max(s, axis=-1, keepdims=True))
        alpha = jnp.exp(m - mn)
        p = jnp.exp(s - mn)
        m_s[...] = mn
        l_s[...] = alpha * l_s[...] + jnp.sum(p, axis=-1, keepdims=True)
        pv = []
        for h in range(N_HEADS):
            ph = p[h * hr:(h + 1) * hr]
            terms = [jnp.dot(ph[:, i * psz:(i + 1) * psz].astype(BF16), head_rows(vp, h),
                             preferred_element_type=F32) for i, vp in enumerate(v_pages)]
            pv.append(functools.reduce(lambda a, b: a + b, terms))
        acc_s[...] = alpha * acc_s[...] + jnp.concatenate(pv, axis=0)

    update(k_refs, v_refs, bias_ref[j])

    @pl.when(j == pl.num_programs(1) - 1)
    def _():
        update([knew_ref], [vnew_ref], bnew_ref[...])
        lam = _lambda(lq1, lk1, lq2, lk2, lam_init)
        o = acc_s[...] / l_s[...]
        for h in range(N_HEADS):
            r0 = h * hr
            od = o[r0:r0 + t_new] - lam * o[r0 + t_new:r0 + hr]
            o_ref[:, h * V_DIM:(h + 1) * V_DIM] = _head_norm(od, sg_ref[...], lam_init)


def _sample_attention(page_table, q_heads, bias_past, bias_new, k_new, v_new, cache_k, cache_v, lams, sg,
                      t_new, lam_init):
    b, n_pages = page_table.shape
    pages = PAGES_PER_STEP
    nj = n_pages // pages
    page_rows = cache_k.shape[1]
    psz = page_rows // N_HEADS
    rows = 2 * N_HEADS * t_new
    lam_spec = _resident((1, HEAD_DIM))

    def page_spec(p):
        return pl.BlockSpec((None, page_rows, V_DIM), lambda i, j, pt: (pt[i, j * pages + p], 0, 0))

    new_spec = pl.BlockSpec((None, page_rows, V_DIM), lambda i, j, pt: (i, 0, 0))
    grid_spec = pltpu.PrefetchScalarGridSpec(
        num_scalar_prefetch=1,
        grid=(b, nj),
        in_specs=[pl.BlockSpec((None, N_HEADS, 2 * t_new, V_DIM), lambda i, j, pt: (i, 0, 0, 0)),
                  _resident(bias_past.shape), _resident(bias_new.shape), new_spec, new_spec,
                  lam_spec, lam_spec, lam_spec, lam_spec, _resident((1, V_DIM))]
                 + [page_spec(p) for p in range(pages)] + [page_spec(p) for p in range(pages)],
        out_specs=pl.BlockSpec((None, t_new, N_HEADS * V_DIM), lambda i, j, pt: (i, 0, 0)),
        scratch_shapes=[pltpu.VMEM((rows, 1), F32), pltpu.VMEM((rows, 1), F32),
                        pltpu.VMEM((rows, V_DIM), F32)],
    )
    return pl.pallas_call(
        functools.partial(_sattn_body, pages=pages, t_new=t_new, psz=psz, lam_init=lam_init),
        grid_spec=grid_spec,
        out_shape=jax.ShapeDtypeStruct((b, t_new, N_HEADS * V_DIM), F32),
        compiler_params=_params("parallel", "arbitrary"),
    )(page_table, q_heads, bias_past, bias_new, k_new, v_new, *lams, sg,
      *([cache_k] * pages), *([cache_v] * pages))


def _merge_body(x_ref, po_ref, ao_ref, gp_ref, ga_ref, wbp_ref, wba_ref, wout_ref, h_ref):
    a = jnp.dot(po_ref[...].astype(BF16), wbp_ref[...], preferred_element_type=F32)
    b = jnp.dot(ao_ref[...].astype(BF16), wba_ref[...], preferred_element_type=F32)
    m = gp_ref[...] * a + ga_ref[...] * b
    h_ref[...] = x_ref[...] + jnp.dot(m.astype(BF16), wout_ref[...], preferred_element_type=F32)


def _merge(x, pool_o, attn_o, gp, ga, wbp, wba, wout):
    n, d = x.shape
    tm = TOK_TILE
    row = lambda c: pl.BlockSpec((tm, c), lambda i: (i, 0))
    return pl.pallas_call(
        _merge_body,
        grid=(n // tm,),
        in_specs=[row(d), row(pool_o.shape[1]), row(attn_o.shape[1]), row(d), row(d),
                  _resident(wbp.shape), _resident(wba.shape), _resident(wout.shape)],
        out_specs=row(d),
        out_shape=jax.ShapeDtypeStruct((n, d), F32),
        compiler_params=_params("parallel"),
    )(x, pool_o, attn_o, gp, ga, wbp, wba, wout)


def _extract_topk(x, key, k):
    big = jnp.iinfo(jnp.int32).max
    vals, keys = [], []
    for _ in range(k):
        m = jnp.max(x, axis=0, keepdims=True)
        kmin = jnp.min(jnp.where(x == m, key, big), axis=0, keepdims=True)
        x = jnp.where(key == kmin, -jnp.inf, x)
        vals.append(m)
        keys.append(kmin)
    return jnp.concatenate(vals, axis=0), jnp.concatenate(keys, axis=0)


def _route_body(h_ref, g_ref, wq_ref, sk_ref, xn_ref, eid_ref, gate_ref, q_scr, sv_scr, si_scr,
                eid_scr, gate_scr, *, tm):
    xn = _rms(h_ref[...], g_ref[...])
    xn_ref[...] = xn
    q = jnp.dot(xn.astype(BF16), wq_ref[...], preferred_element_type=F32)
    n_lists = 2 * PEER_HEADS
    for l in range(n_lists):
        q_scr[l] = q[:, l * PEER_HALF:(l + 1) * PEER_HALF].astype(BF16)
    groups = tm // LANES
    key_iota = lax.broadcasted_iota(I32, (N_KEYS, LANES), 0)

    def list_body(i, carry):
        for l in [LISTS_PER_ITER * i + k for k in range(LISTS_PER_ITER)]:
            st = lax.dot_general(sk_ref[l], q_scr[l], _NT, preferred_element_type=F32)
            for g in range(groups):
                cs = slice(g * LANES, (g + 1) * LANES)
                vals, idx = _extract_topk(st[:, cs], key_iota, PEER_TOPK)
                sv_scr[l, :, cs] = vals
                si_scr[l, :, cs] = idx
        return carry

    lax.fori_loop(0, n_lists // LISTS_PER_ITER, list_body, 0)

    j_iota = lax.broadcasted_iota(I32, (PEER_TOPK, LANES), 0)

    def head_body(hh, carry):
        rows = pl.ds(pl.multiple_of(hh * PEER_TOPK, PEER_TOPK), PEER_TOPK)
        for g in range(groups):
            cs = slice(g * LANES, (g + 1) * LANES)
            a = sv_scr[2 * hh, :, cs]
            b = sv_scr[2 * hh + 1, :, cs]
            ia = si_scr[2 * hh, :, cs]
            ib = si_scr[2 * hh + 1, :, cs]
            kb = (j_iota << KEY_SHIFT) + ib
            row_key = lambda i: ia[i:i + 1] * N_KEYS + ((i * PEER_TOPK) << KEY_SHIFT)
            half = PEER_TOPK // 2
            cands = [a[0:1] + b]
            keys = [kb + row_key(0)]
            for i in range(1, half):
                valid = j_iota[:half] < PEER_TOPK // (i + 1)
                cands.append(jnp.where(valid, a[i:i + 1] + b[:half], -jnp.inf))
                keys.append(kb[:half] + row_key(i))
            cands.append(a[half:] + b[0:1])
            keys.append(((j_iota[half:] * PEER_TOPK) << KEY_SHIFT) + ia[half:] * N_KEYS + ib[0:1])
            top_s, top_key = _extract_topk(jnp.concatenate(cands, axis=0), jnp.concatenate(keys, axis=0),
                                           PEER_TOPK)
            e = jnp.exp(top_s - top_s[0:1])
            eid_scr[rows, cs] = (top_key & ((1 << KEY_SHIFT) - 1)) * ROW_WORDS
            gate_scr[rows, cs] = e / jnp.sum(e, axis=0, keepdims=True)
        return carry

    lax.fori_loop(0, PEER_HEADS, head_body, 0)
    eid_ref[...] = eid_scr[...].T
    gate_ref[...] = gate_scr[...].T


def _route(h, g, wq_bf, sk_bf):
    n, d = h.shape
    tm = TOK_TILE
    n_lists = 2 * PEER_HEADS
    row = lambda c: pl.BlockSpec((tm, c), lambda i: (i, 0))
    return pl.pallas_call(
        functools.partial(_route_body, tm=tm),
        grid=(n // tm,),
        in_specs=[row(d), _resident((1, d)), _resident(wq_bf.shape), _resident(sk_bf.shape)],
        out_specs=[row(d), row(N_SLOTS), row(N_SLOTS)],
        out_shape=[jax.ShapeDtypeStruct((n, d), F32), jax.ShapeDtypeStruct((n, N_SLOTS), I32),
                   jax.ShapeDtypeStruct((n, N_SLOTS), F32)],
        scratch_shapes=[pltpu.VMEM((n_lists, tm, PEER_HALF), BF16),
                        pltpu.VMEM((n_lists, PEER_TOPK, tm), F32),
                        pltpu.VMEM((n_lists, PEER_TOPK, tm), I32),
                        pltpu.VMEM((N_SLOTS, tm), I32),
                        pltpu.VMEM((N_SLOTS, tm), F32)],
        compiler_params=_params("parallel"),
    )(h, g, wq_bf, sk_bf)


CHUNKS = 2 * ROW_WORDS
GB_ROWS = N_SLOTS * CHUNKS
N_GBUF = 2
UPASS_TOKENS_PER_ITER = 16
VPASS_TOKENS_PER_ITER = 32
_GBUFS = [pltpu.VMEM((N_SLOTS * ROW_WORDS, LANES), I32)] * N_GBUF


PACK_TILE = 512


def _bf16_bits(x):
    b = lax.bitcast_convert_type(x, jnp.uint32)
    return (b + 0x7FFF + ((b >> 16) & 1)) & jnp.uint32(0xFFFF0000)


def _pack_body(t_ref, o_ref, *, rows):
    for s in range(ROW_WORDS):
        lo = _bf16_bits(t_ref[:, (2 * s) * LANES:(2 * s + 1) * LANES])
        hi = _bf16_bits(t_ref[:, (2 * s + 1) * LANES:(2 * s + 2) * LANES])
        o_ref[pl.ds(s, rows, stride=ROW_WORDS), :] = lax.bitcast_convert_type(hi | (lo >> 16), I32)


def _pack_table(tab):
    n, d = tab.shape
    assert d == CHUNKS * LANES
    return pl.pallas_call(
        functools.partial(_pack_body, rows=PACK_TILE),
        grid=(n // PACK_TILE,),
        in_specs=[pl.BlockSpec((PACK_TILE, d), lambda i: (i, 0))],
        out_specs=pl.BlockSpec((PACK_TILE * ROW_WORDS, LANES), lambda i: (i, 0)),
        out_shape=jax.ShapeDtypeStruct((n * ROW_WORDS, LANES), I32),
        compiler_params=_params("parallel"),
    )(tab)


def _stage_rows(eid_ref, t, tab_ref, gbuf):
    for r in range(N_SLOTS):
        off = pl.multiple_of(eid_ref[t, r], ROW_WORDS)
        gbuf[r * ROW_WORDS:(r + 1) * ROW_WORDS, :] = tab_ref[pl.ds(off, ROW_WORDS), :]


def _staged(gbuf):
    return pltpu.bitcast(gbuf[...], BF16)


def _for_each_token(tb, per_iter, eid_ref, tab_ref, gbufs, start, finish):
    _stage_rows(eid_ref, 0, tab_ref, gbufs[0])

    def group(i, carry):
        t0 = per_iter * i
        for j in range(per_iter):
            y = start(t0 + j, _staged(gbufs[j % 2]))
            nxt = t0 + j + 1
            if j == per_iter - 1:
                nxt = jnp.minimum(nxt, tb - 1)
            _stage_rows(eid_ref, nxt, tab_ref, gbufs[(j + 1) % 2])
            finish(t0 + j, y)
        return carry

    lax.fori_loop(0, tb // per_iter, group, 0)


def _chunk_mask():
    k = lax.broadcasted_iota(I32, (CHUNKS, GB_ROWS), 1)
    q = lax.broadcasted_iota(I32, (CHUNKS, GB_ROWS), 0)
    return (k & (CHUNKS - 1)) == q


def _token_rows(t):
    return pl.ds(pl.multiple_of(t * CHUNKS, CHUNKS), CHUNKS)


def _upass_body(eid_ref, xn_ref, gate_ref, tab_ref, rexp_ref, rsum_ref, w_ref, *scratch, tb):
    gbufs, (xbuf, zbuf) = scratch[:N_GBUF], scratch[N_GBUF:]
    for q in range(CHUNKS):
        xbuf[pl.ds(q, tb, stride=CHUNKS), :] = xn_ref[:, q * LANES:(q + 1) * LANES]
    mask = _chunk_mask()

    def start(t, g):
        x8 = xbuf[_token_rows(t), :].astype(BF16)
        return lax.dot_general(x8, g, _NT, preferred_element_type=F32)

    def finish(t, y):
        zbuf[pl.ds(t, 1), :] = jnp.sum(jnp.where(mask, y, 0.0), axis=0, keepdims=True)

    _for_each_token(tb, UPASS_TOKENS_PER_ITER, eid_ref, tab_ref, gbufs, start, finish)
    a = jnp.dot(zbuf[...], rsum_ref[...], preferred_element_type=F32, precision=lax.Precision.HIGHEST)
    act = 0.5 * a * (1.0 + lax.erf(a * math.sqrt(0.5)))
    w = (gate_ref[...] * act).astype(BF16)
    w_ref[...] = jnp.dot(w, rexp_ref[...], preferred_element_type=F32)


def _table_spec(tab):
    return pl.BlockSpec(tab.shape, lambda i: (0, 0), pipeline_mode=pl.Buffered(1))


def _gather_specs(d):
    tb = GATHER_TILE
    smem = pl.BlockSpec((tb, N_SLOTS), lambda i: (i, 0), memory_space=pltpu.SMEM)
    return tb, smem, (lambda c: pl.BlockSpec((tb, c), lambda i: (i, 0)))


def _upass(eid, xn, gate, tab, rexp, rsum):
    n, d = xn.shape
    tb, smem, row = _gather_specs(d)
    return pl.pallas_call(
        functools.partial(_upass_body, tb=tb),
        grid=(n // tb,),
        in_specs=[smem, row(d), row(N_SLOTS), _table_spec(tab), _resident(rexp.shape), _resident(rsum.shape)],
        out_specs=row(GB_ROWS),
        out_shape=jax.ShapeDtypeStruct((n, GB_ROWS), F32),
        scratch_shapes=_GBUFS + [
                        pltpu.VMEM((tb * CHUNKS, LANES), F32),
                        pltpu.VMEM((tb, GB_ROWS), F32)],
        compiler_params=_params("parallel"),
    )(eid, xn, gate, tab, rexp, rsum)


def _vpass_body(eid_ref, w_ref, h_ref, g_ref, tab_ref, y_ref, *scratch, tb):
    gbufs, pbuf = scratch[:N_GBUF], scratch[N_GBUF]
    mask = _chunk_mask()

    def start(t, g):
        wsel = jnp.where(mask, w_ref[pl.ds(t, 1), :], 0.0).astype(BF16)
        return jnp.dot(wsel, g, preferred_element_type=F32)

    def finish(t, y):
        pbuf[_token_rows(t), :] = y

    _for_each_token(tb, VPASS_TOKENS_PER_ITER, eid_ref, tab_ref, gbufs, start, finish)
    peer =jnp.concatenate([pbuf[pl.ds(q, tb, stride=CHUNKS), :] for q in range(CHUNKS)], axis=1)
    y_ref[...] = _rms(h_ref[...] + peer, g_ref[...])


def _vpass(eid, w, h, gfin, tab):
    n, d = h.shape
    tb, smem, row = _gather_specs(d)
    return pl.pallas_call(
        functools.partial(_vpass_body, tb=tb),
        grid=(n // tb,),
        in_specs=[smem, row(GB_ROWS), row(d), _resident((1, d)), _table_spec(tab)],
        out_specs=row(d),
        out_shape=jax.ShapeDtypeStruct((n, d), F32),
        scratch_shapes=_GBUFS + [pltpu.VMEM((tb * CHUNKS, LANES), F32)],
        compiler_params=_params("parallel"),
    )(eid, w, h, gfin, tab)


def _peer_and_final_norm(h, g_ffn, wq_bf, sk_bf, u_tab, v_tab, g_final):
    xn, eid, gate = _route(h, g_ffn, wq_bf, sk_bf)
    slot_of_row = jnp.arange(GB_ROWS, dtype=I32) // CHUNKS
    rexp = (jnp.arange(N_SLOTS, dtype=I32)[:, None] == slot_of_row[None, :])
    w = _upass(eid, xn, gate, u_tab, rexp.astype(BF16), rexp.T.astype(F32))
    return _vpass(eid, w, h, g_final.reshape(1, -1), v_tab)


def kernel(x_prompt, x_sample, cache_k, cache_v, state_pool, page_table, norm_mix_g, w_in, pool_group_w,
           pool_scale, lambda_q1, lambda_k1, lambda_q2, lambda_k2, subln_g, rel_bias, w_branch_pool,
           w_branch_attn, w_out, norm_ffn_g, peer_wq, peer_subkeys, peer_u, peer_v, norm_final_g):
    depth = w_in.shape[0]
    assert depth == 1
    b_p, s_p, d = x_prompt.shape
    b_s, t_s, _ = x_sample.shape
    psz = cache_k.shape[2]
    past = page_table.shape[1] * psz
    width = N_HEADS * V_DIM
    assert ATT_BLOCK >= MAX_DISTANCE and s_p % ATT_BLOCK == 0
    l = 0
    lam_init = 0.8 - 0.6 * math.exp(-0.3 * l)

    w_in_bf = w_in[l].astype(BF16)
    gw_bf = pool_group_w[l].astype(BF16)
    wbp, wba, wout = (w[l].astype(BF16) for w in (w_branch_pool, w_branch_attn, w_out))
    wq_bf = peer_wq[l].astype(BF16)
    sk_bf = peer_subkeys[l].astype(BF16).reshape(2 * PEER_HEADS, N_KEYS, PEER_HALF)
    u_tab = _pack_table(peer_u.reshape(-1, d))
    v_tab = _pack_table(peer_v.reshape(-1, d))
    g_mix = norm_mix_g[l].reshape(1, d)
    g_ffn = norm_ffn_g[l].reshape(1, d)
    scale = pool_scale[l].reshape(1, -1)
    sg = subln_g[l].reshape(1, V_DIM)
    lams = tuple(a[l].astype(F32).reshape(1, HEAD_DIM) for a in (lambda_q1, lambda_k1, lambda_q2, lambda_k2))

    r = jnp.arange(ATT_BLOCK, dtype=I32)
    dist = jnp.stack([dj * ATT_BLOCK + r[:, None] - r[None, :] for dj in range(3)])
    bias_tiles = _rel_bias(rel_bias, dist)

    tq = jnp.arange(t_s, dtype=I32)
    dist_past = past + tq[:, None] - jnp.arange(past, dtype=I32)[None, :]
    rows = 2 * N_HEADS * t_s
    per_map = lambda b: jnp.broadcast_to(b[:, None], (N_HEADS, 2) + b.shape[1:]).reshape(rows, -1)
    step_keys = PAGES_PER_STEP * psz
    bias_past = per_map(_rel_bias(rel_bias, dist_past))
    bias_past = bias_past.reshape(rows, past // step_keys, step_keys).transpose(1, 0, 2)
    dist_new = tq[:, None] - jnp.arange(psz, dtype=I32)[None, :]
    dist_new = jnp.where(jnp.arange(psz)[None, :] < t_s, dist_new, -1)
    bias_new = per_map(_rel_bias(rel_bias, dist_new))

    def token_stages(x, hist, pos0, attention):
        b, length, _ = x.shape
        n = b * length
        u, q, k, v, gp, ga = _inproj(x.reshape(n, d), g_mix, w_in_bf)
        pool_o = _pool(hist, u.reshape(b, length, -1), gw_bf, scale, pos0)
        attn_o = attention(q, k, v)
        h = _merge(x.reshape(n, d), pool_o.reshape(n, -1), attn_o, gp, ga, wbp, wba, wout)
        y = _peer_and_final_norm(h, g_ffn, wq_bf, sk_bf, u_tab, v_tab, norm_final_g)
        return y.reshape(b, length, d), u, k, v

    hist_p = jnp.zeros((b_p, HIST_ROWS, width), F32)
    attn_p = lambda q, k, v: _prompt_attention(q, k, v, bias_tiles, lams, sg, b_p, s_p, lam_init)
    y_p, u_p, k_p, v_p = token_stages(x_prompt, hist_p, 0, attn_p)

    hist_s = jnp.pad(state_pool[l], ((0, 0), (HIST_ROWS - POOL_STATE, 0), (0, 0)))
    ck = cache_k.reshape(-1, psz * N_HEADS, V_DIM)
    cv = cache_v.reshape(-1, psz * N_HEADS, V_DIM)

    def attn_s(q, k, v):
        q4 = q.reshape(b_s, t_s, N_HEADS, V_DIM).transpose(0, 2, 1, 3)
        first = jnp.arange(V_DIM) < HEAD_DIM
        zero = jnp.zeros((), BF16)
        q_heads = jnp.concatenate([jnp.where(first, q4, zero), jnp.where(first, zero, q4)], axis=2)
        pad = ((0, 0), (0, (psz - t_s) * N_HEADS), (0, 0))
        k_new = jnp.pad(k.reshape(b_s, t_s * N_HEADS, V_DIM), pad)
        v_new = jnp.pad(v.reshape(b_s, t_s * N_HEADS, V_DIM), pad)
        o = _sample_attention(page_table, q_heads, bias_past, bias_new, k_new, v_new, ck, cv, lams, sg,
                              t_s, lam_init)
        return o.reshape(b_s * t_s, width)

    y_s, u_s, k_s, v_s = token_stages(x_sample, hist_s, past, attn_s)

    kv_shape = lambda b, length: (1, b, length, N_HEADS, V_DIM)
    u_p3 = u_p.reshape(b_p, s_p, width)
    u_s3 = u_s.reshape(b_s, t_s, width)
    pool_prompt = u_p3[:, -POOL_STATE:][None]
    pool_sample = jnp.concatenate([state_pool[l], u_s3], axis=1)[:, -POOL_STATE:][None]
    return (y_p, y_s, k_p.reshape(kv_shape(b_p, s_p)), v_p.reshape(kv_shape(b_p, s_p)), pool_prompt,
            k_s.reshape(kv_shape(b_s, t_s)), v_s.reshape(kv_shape(b_s, t_s)), pool_sample)
```

```python
import functools
import math

import jax
import jax.numpy as jnp
from jax import lax
from jax.experimental import pallas as pl
from jax.experimental.pallas import tpu as pltpu

F32 = jnp.float32
BF16 = jnp.bfloat16
I32 = jnp.int32

EPS = 1e-6
NEG_INF = -1e30

LANES = 128
SUBLANES = 8
VMEM_BYTES = 64 * 1024 * 1024
VMEM_LIMIT_BYTES = VMEM_BYTES - 8 * 1024 * 1024

POOL_WINDOWS = (2, 4, 8, 16)
POOL_STATE = max(POOL_WINDOWS) - 1
HIST_ROWS = -(-POOL_STATE // SUBLANES) * SUBLANES
N_HEADS = 4
HEAD_DIM = 64
V_DIM = 2 * HEAD_DIM
MIX_WIDTH = N_HEADS * V_DIM
N_BUCKETS = 32
MAX_DISTANCE = 128
PEER_HEADS = 8
PEER_TOPK = 16
N_KEYS = 128
PEER_HALF = 128
N_SLOTS = PEER_HEADS * PEER_TOPK
ROW_WORDS = 4
KEY_SHIFT = (N_KEYS * N_KEYS - 1).bit_length()

TOK_TILE = 256
ATT_BLOCK = 512
PAGES_PER_STEP = 32
GATHER_TILE = 128
LISTS_PER_ITER = 8

_NT = (((1,), (1,)), ((), ()))


def _params(*sem):
    return pltpu.CompilerParams(dimension_semantics=sem, vmem_limit_bytes=VMEM_LIMIT_BYTES)


def _rms(x, g):
    return (x * lax.rsqrt(jnp.mean(x * x, axis=-1, keepdims=True) + EPS)) * g


def _resident(shape):
    nd = len(shape)
    return pl.BlockSpec(shape, lambda *_: (0,) * nd)


def _inproj_body(x_ref, g_ref, w_ref, u_ref, q_ref, k_ref, v_ref, gp_ref, ga_ref, *, splits):
    xb = _rms(x_ref[...], g_ref[...]).astype(BF16)

    def proj(i):
        return jnp.dot(xb, w_ref[:, splits[i]:splits[i + 1]], preferred_element_type=F32)

    u_ref[...] = proj(0)
    q_ref[...] = (proj(1) * (HEAD_DIM ** -0.5)).astype(BF16)
    k_ref[...] = proj(2)
    v_ref[...] = proj(3)
    gp_ref[...] = jax.nn.sigmoid(proj(4))
    ga_ref[...] = jax.nn.sigmoid(proj(5))


def _inproj(x, g, w_in_bf):
    n, d = x.shape
    pw = MIX_WIDTH
    assert w_in_bf.shape == (d, 4 * pw + 2 * d)
    splits = (0, pw, 2 * pw, 3 * pw, 4 * pw, 4 * pw + d, 4 * pw + 2 * d)
    tm = TOK_TILE
    row = lambda c: pl.BlockSpec((tm, c), lambda i: (i, 0))
    return pl.pallas_call(
        functools.partial(_inproj_body, splits=splits),
        grid=(n // tm,),
        in_specs=[row(d), _resident((1, d)), _resident(w_in_bf.shape)],
        out_specs=[row(pw), row(pw), row(pw), row(pw), row(d), row(d)],
        out_shape=[jax.ShapeDtypeStruct((n, pw), F32), jax.ShapeDtypeStruct((n, pw), BF16),
                   jax.ShapeDtypeStruct((n, pw), F32), jax.ShapeDtypeStruct((n, pw), F32),
                   jax.ShapeDtypeStruct((n, d), F32), jax.ShapeDtypeStruct((n, d), F32)],
        compiler_params=_params("parallel"),
    )(x, g, w_in_bf)


def _pool_body(hist_ref, u_ref, gw_ref, sc_ref, o_ref, buf, *, length, pos0):
    buf[0:HIST_ROWS, :] = hist_ref[...]
    buf[HIST_ROWS:HIST_ROWS + length, :] = u_ref[...]
    t = lax.broadcasted_iota(I32, (length, 1), 0)
    for gi, w in enumerate(POOL_WINDOWS):
        c0 = gi * LANES
        ug = buf[HIST_ROWS:HIST_ROWS + length, c0:c0 + LANES]
        acc = ug
        for j in range(1, w):
            acc = acc + buf[HIST_ROWS - j:HIST_ROWS - j + length, c0:c0 + LANES]
        cnt = jnp.minimum(t + (pos0 + 1), w).astype(F32)
        mixed = acc / cnt - ug
        y = jnp.dot(mixed.astype(BF16), gw_ref[gi], preferred_element_type=F32)
        o_ref[:, c0:c0 + LANES] = y * sc_ref[:, c0:c0 + LANES]


def _pool(hist, u, gw_bf, scale, pos0):
    b, length, pw = u.shape
    return pl.pallas_call(
        functools.partial(_pool_body, length=length, pos0=pos0),
        grid=(b,),
        in_specs=[pl.BlockSpec((None, HIST_ROWS, pw), lambda i: (i, 0, 0)),
                  pl.BlockSpec((None, length, pw), lambda i: (i, 0, 0)),
                  _resident(gw_bf.shape), _resident((1, pw))],
        out_specs=pl.BlockSpec((None, length, pw), lambda i: (i, 0, 0)),
        out_shape=jax.ShapeDtypeStruct((b, length, pw), F32),
        scratch_shapes=[pltpu.VMEM((HIST_ROWS + length, pw), F32)],
        compiler_params=_params("parallel"),
    )(hist, u, gw_bf, scale)


def _lambda(lq1, lk1, lq2, lk2, lam_init):
    d1 = jnp.sum(lq1[...] * lk1[...], axis=-1, keepdims=True)
    d2 = jnp.sum(lq2[...] * lk2[...], axis=-1, keepdims=True)
    return jnp.exp(d1) - jnp.exp(d2) + lam_init


def _softmax_update(s, vals, m, l, acc):
    mn = jnp.maximum(m, jnp.max(s, axis=-1, keepdims=True))
    alpha = jnp.exp(m - mn)
    p = jnp.exp(s - mn)
    l = alpha * l + jnp.sum(p, axis=-1, keepdims=True)
    pb = p.astype(BF16)
    pv = None
    for (c0, c1), vb in vals:
        term = jnp.dot(pb[:, c0:c1], vb, preferred_element_type=F32)
        pv = term if pv is None else pv + term
    return mn, l, alpha * acc + pv


def _head_norm(o, sg, lam_init):
    return _rms(o, sg) * (1.0 - lam_init)


def _t5_bucket(n):
    max_exact = N_BUCKETS // 2
    nf = jnp.maximum(n, 1).astype(F32)
    large = max_exact + (jnp.log(nf / max_exact) / math.log(MAX_DISTANCE / max_exact)
                         * (N_BUCKETS - max_exact)).astype(I32)
    large = jnp.minimum(large, N_BUCKETS - 1)
    return jnp.where(n < max_exact, n, large)


def _rel_bias(rel_bias, n):
    bucket = _t5_bucket(jnp.maximum(n, 0))
    table = rel_bias.astype(F32)
    b = jnp.zeros((table.shape[1],) + n.shape, F32)
    for i in range(N_BUCKETS):
        b = jnp.where(bucket == i, table[i].reshape((-1,) + (1,) * n.ndim), b)
    return jnp.where(n >= 0, b, NEG_INF)


def _pattn_body(q_ref, k_ref, v_ref, bias_ref, lq1, lk1, lq2, lk2, sg_ref, o_ref, kb, vb,
                *, blk, lam_init):
    i = pl.program_id(2)

    @pl.when(i == 0)
    def _():
        kb[...] = k_ref[...].astype(BF16)
        vb[...] = v_ref[...].astype(BF16)

    q = q_ref[...]
    lane = lax.broadcasted_iota(I32, q.shape, 1)
    zero = jnp.zeros_like(q)
    q1 = jnp.where(lane < HEAD_DIM, q, zero)
    q2 = jnp.where(lane < HEAD_DIM, zero, q)

    def step(j, carry):
        m1, l1, a1, m2, l2, a2 = carry
        off = pl.multiple_of(j * blk, blk)
        kc = kb[pl.ds(off, blk), :]
        vc = vb[pl.ds(off, blk), :]
        bt = bias_ref[jnp.minimum(i - j, 2)]
        s1 = lax.dot_general(q1, kc, _NT, preferred_element_type=F32) + bt
        s2 = lax.dot_general(q2, kc, _NT, preferred_element_type=F32) + bt
        vals = [((0, blk), vc)]
        m1, l1, a1 = _softmax_update(s1, vals, m1, l1, a1)
        m2, l2, a2 = _softmax_update(s2, vals, m2, l2, a2)
        return m1, l1, a1, m2, l2, a2

    col = lambda v: jnp.full((blk, 1), v, F32)
    acc0 = jnp.zeros((blk, V_DIM), F32)
    init = (col(-jnp.inf), col(0.0), acc0, col(-jnp.inf), col(0.0), acc0)
    _, l1, a1, _, l2, a2 = lax.fori_loop(0, i + 1, step, init)
    lam = _lambda(lq1, lk1, lq2, lk2, lam_init)
    o = a1 / l1 - lam * (a2 / l2)
    o_ref[...] = _head_norm(o, sg_ref[...], lam_init)


def _prompt_attention(q, k, v, bias_tiles, lams, sg, batch, seq, lam_init):
    blk = ATT_BLOCK
    nq = seq // blk
    hw = V_DIM
    lam_spec = _resident((1, HEAD_DIM))
    return pl.pallas_call(
        functools.partial(_pattn_body, blk=blk, lam_init=lam_init),
        grid=(batch, N_HEADS, nq),
        in_specs=[pl.BlockSpec((blk, hw), lambda b, h, i: (b * nq + i, h)),
                  pl.BlockSpec((seq, hw), lambda b, h, i: (b, h)),
                  pl.BlockSpec((seq, hw), lambda b, h, i: (b, h)),
                  pl.BlockSpec((None, 3, blk, blk), lambda b, h, i: (h, 0, 0, 0)),
                  lam_spec, lam_spec, lam_spec, lam_spec, _resident((1, hw))],
        out_specs=pl.BlockSpec((blk, hw), lambda b, h, i: (b * nq + i, h)),
        out_shape=jax.ShapeDtypeStruct((batch * seq, N_HEADS * hw), F32),
        scratch_shapes=[pltpu.VMEM((seq, hw), BF16), pltpu.VMEM((seq, hw), BF16)],
        compiler_params=_params("parallel", "parallel", "arbitrary"),
    )(q, k, v, bias_tiles, *lams, sg)


def _sattn_body(pt_ref, q_ref, bias_ref, bnew_ref, knew_ref, vnew_ref, lq1, lk1, lq2, lk2, sg_ref,
                *rest, pages, t_new, psz, lam_init):
    k_refs = rest[:pages]
    v_refs = rest[pages:2 * pages]
    o_ref, m_s, l_s, acc_s = rest[2 * pages:]
    j = pl.program_id(1)
    hr = 2 * t_new

    @pl.when(j == 0)
    def _():
        m_s[...] = jnp.full(m_s.shape, -jnp.inf, F32)
        l_s[...] = jnp.zeros(l_s.shape, F32)
        acc_s[...] = jnp.zeros(acc_s.shape, F32)

    def head_rows(ref, h):
        return ref[pl.ds(h, psz, stride=N_HEADS), :].astype(BF16)

    def update(k_pages, v_pages, bias):
        s = jnp.concatenate(
            [jnp.concatenate([lax.dot_general(q_ref[h], head_rows(kp, h), _NT, preferred_element_type=F32)
                              for kp in k_pages], axis=1) for h in range(N_HEADS)], axis=0) + bias
        m = m_s[...]
        mn = jnp.maximum(m, jnp.max(s, axis=-1, keepdims=True))
        alpha = jnp.exp(m - mn)
        p = jnp.exp(s - mn)
        m_s[...] = mn
        l_s[...] = alpha * l_s[...] + jnp.sum(p, axis=-1, keepdims=True)
        pv = []
        for h in range(N_HEADS):
            ph = p[h * hr:(h + 1) * hr]
            terms = [jnp.dot(ph[:, i * psz:(i + 1) * psz].astype(BF16), head_rows(vp, h),
                             preferred_element_type=F32) for i, vp in enumerate(v_pages)]
            pv.append(functools.reduce(lambda a, b: a + b, terms))
        acc_s[...] = alpha * acc_s[...] + jnp.concatenate(pv, axis=0)

    update(k_refs, v_refs, bias_ref[j])

    @pl.when(j == pl.num_programs(1) - 1)
    def _():
        update([knew_ref], [vnew_ref], bnew_ref[...])
        lam = _lambda(lq1, lk1, lq2, lk2, lam_init)
        o = acc_s[...] / l_s[...]
        for h in range(N_HEADS):
            r0 = h * hr
            od = o[r0:r0 + t_new] - lam * o[r0 + t_new:r0 + hr]
            o_ref[:, h * V_DIM:(h + 1) * V_DIM] = _head_norm(od, sg_ref[...], lam_init)


def _sample_attention(page_table, q_heads, bias_past, bias_new, k_new, v_new, cache_k, cache_v, lams, sg,
                      t_new, lam_init):
    b, n_pages = page_table.shape
    pages = PAGES_PER_STEP
    nj = n_pages // pages
    page_rows = cache_k.shape[1]
    psz = page_rows // N_HEADS
    rows = 2 * N_HEADS * t_new
    lam_spec = _resident((1, HEAD_DIM))

    def page_spec(p):
        return pl.BlockSpec((None, page_rows, V_DIM), lambda i, j, pt: (pt[i, j * pages + p], 0, 0))

    new_spec = pl.BlockSpec((None, page_rows, V_DIM), lambda i, j, pt: (i, 0, 0))
    grid_spec = pltpu.PrefetchScalarGridSpec(
        num_scalar_prefetch=1,
        grid=(b, nj),
        in_specs=[pl.BlockSpec((None, N_HEADS, 2 * t_new, V_DIM), lambda i, j, pt: (i, 0, 0, 0)),
                  _resident(bias_past.shape), _resident(bias_new.shape), new_spec, new_spec,
                  lam_spec, lam_spec, lam_spec, lam_spec, _resident((1, V_DIM))]
                 + [page_spec(p) for p in range(pages)] + [page_spec(p) for p in range(pages)],
        out_specs=pl.BlockSpec((None, t_new, N_HEADS * V_DIM), lambda i, j, pt: (i, 0, 0)),
        scratch_shapes=[pltpu.VMEM((rows, 1), F32), pltpu.VMEM((rows, 1), F32),
                        pltpu.VMEM((rows, V_DIM), F32)],
    )
    return pl.pallas_call(
        functools.partial(_sattn_body, pages=pages, t_new=t_new, psz=psz, lam_init=lam_init),
        grid_spec=grid_spec,
        out_shape=jax.ShapeDtypeStruct((b, t_new, N_HEADS * V_DIM), F32),
        compiler_params=_params("parallel", "arbitrary"),
    )(page_table, q_heads, bias_past, bias_new, k_new, v_new, *lams, sg,
      *([cache_k] * pages), *([cache_v] * pages))


def _merge_body(x_ref, po_ref, ao_ref, gp_ref, ga_ref, wbp_ref, wba_ref, wout_ref, h_ref):
    a = jnp.dot(po_ref[...].astype(BF16), wbp_ref[...], preferred_element_type=F32)
    b = jnp.dot(ao_ref[...].astype(BF16), wba_ref[...], preferred_element_type=F32)
    m = gp_ref[...] * a + ga_ref[...] * b
    h_ref[...] = x_ref[...] + jnp.dot(m.astype(BF16), wout_ref[...], preferred_element_type=F32)


def _merge(x, pool_o, attn_o, gp, ga, wbp, wba, wout):
    n, d = x.shape
    tm = TOK_TILE
    row = lambda c: pl.BlockSpec((tm, c), lambda i: (i, 0))
    return pl.pallas_call(
        _merge_body,
        grid=(n // tm,),
        in_specs=[row(d), row(pool_o.shape[1]), row(attn_o.shape[1]), row(d), row(d),
                  _resident(wbp.shape), _resident(wba.shape), _resident(wout.shape)],
        out_specs=row(d),
        out_shape=jax.ShapeDtypeStruct((n, d), F32),
        compiler_params=_params("parallel"),
    )(x, pool_o, attn_o, gp, ga, wbp, wba, wout)


def _extract_topk(x, key, k):
    big = jnp.iinfo(jnp.int32).max
    vals, keys = [], []
    for _ in range(k):
        m = jnp.max(x, axis=0, keepdims=True)
        kmin = jnp.min(jnp.where(x == m, key, big), axis=0, keepdims=True)
        x = jnp.where(key == kmin, -jnp.inf, x)
        vals.append(m)
        keys.append(kmin)
    return jnp.concatenate(vals, axis=0), jnp.concatenate(keys, axis=0)


def _route_body(h_ref, g_ref, wq_ref, sk_ref, xn_ref, eid_ref, gate_ref, q_scr, sv_scr, si_scr,
                eid_scr, gate_scr, *, tm):
    xn = _rms(h_ref[...], g_ref[...])
    xn_ref[...] = xn
    q = jnp.dot(xn.astype(BF16), wq_ref[...], preferred_element_type=F32)
    n_lists = 2 * PEER_HEADS
    for l in range(n_lists):
        q_scr[l] = q[:, l * PEER_HALF:(l + 1) * PEER_HALF].astype(BF16)
    groups = tm // LANES
    key_iota = lax.broadcasted_iota(I32, (N_KEYS, LANES), 0)

    def list_body(i, carry):
        for l in [LISTS_PER_ITER * i + k for k in range(LISTS_PER_ITER)]:
            st = lax.dot_general(sk_ref[l], q_scr[l], _NT, preferred_element_type=F32)
            for g in range(groups):
                cs = slice(g * LANES, (g + 1) * LANES)
                vals, idx = _extract_topk(st[:, cs], key_iota, PEER_TOPK)
                sv_scr[l, :, cs] = vals
                si_scr[l, :, cs] = idx
        return carry

    lax.fori_loop(0, n_lists // LISTS_PER_ITER, list_body, 0)

    j_iota = lax.broadcasted_iota(I32, (PEER_TOPK, LANES), 0)

    def head_body(hh, carry):
        rows = pl.ds(pl.multiple_of(hh * PEER_TOPK, PEER_TOPK), PEER_TOPK)
        for g in range(groups):
            cs = slice(g * LANES, (g + 1) * LANES)
            a = sv_scr[2 * hh, :, cs]
            b = sv_scr[2 * hh + 1, :, cs]
            ia = si_scr[2 * hh, :, cs]
            ib = si_scr[2 * hh + 1, :, cs]
            kb = (j_iota << KEY_SHIFT) + ib
            row_key = lambda i: ia[i:i + 1] * N_KEYS + ((i * PEER_TOPK) << KEY_SHIFT)
            half = PEER_TOPK // 2
            cands = [a[0:1] + b]
            keys = [kb + row_key(0)]
            for i in range(1, half):
                valid = j_iota[:half] < PEER_TOPK // (i + 1)
                cands.append(jnp.where(valid, a[i:i + 1] + b[:half], -jnp.inf))
                keys.append(kb[:half] + row_key(i))
            cands.append(a[half:] + b[0:1])
            keys.append(((j_iota[half:] * PEER_TOPK) << KEY_SHIFT) + ia[half:] * N_KEYS + ib[0:1])
            top_s, top_key = _extract_topk(jnp.concatenate(cands, axis=0), jnp.concatenate(keys, axis=0),
                                           PEER_TOPK)
            e = jnp.exp(top_s - top_s[0:1])
            eid_scr[rows, cs] = (top_key & ((1 << KEY_SHIFT) - 1)) * ROW_WORDS
            gate_scr[rows, cs] = e / jnp.sum(e, axis=0, keepdims=True)
        return carry

    lax.fori_loop(0, PEER_HEADS, head_body, 0)
    eid_ref[...] = eid_scr[...].T
    gate_ref[...] = gate_scr[...].T


def _route(h, g, wq_bf, sk_bf):
    n, d = h.shape
    tm = TOK_TILE
    n_lists = 2 * PEER_HEADS
    row = lambda c: pl.BlockSpec((tm, c), lambda i: (i, 0))
    return pl.pallas_call(
        functools.partial(_route_body, tm=tm),
        grid=(n // tm,),
        in_specs=[row(d), _resident((1, d)), _resident(wq_bf.shape), _resident(sk_bf.shape)],
        out_specs=[row(d), row(N_SLOTS), row(N_SLOTS)],
        out_shape=[jax.ShapeDtypeStruct((n, d), F32), jax.ShapeDtypeStruct((n, N_SLOTS), I32),
                   jax.ShapeDtypeStruct((n, N_SLOTS), F32)],
        scratch_shapes=[pltpu.VMEM((n_lists, tm, PEER_HALF), BF16),
                        pltpu.VMEM((n_lists, PEER_TOPK, tm), F32),
                        pltpu.VMEM((n_lists, PEER_TOPK, tm), I32),
                        pltpu.VMEM((N_SLOTS, tm), I32),
                        pltpu.VMEM((N_SLOTS, tm), F32)],
        compiler_params=_params("parallel"),
    )(h, g, wq_bf, sk_bf)


CHUNKS = 2 * ROW_WORDS
GB_ROWS = N_SLOTS * CHUNKS
N_GBUF = 2
TOKENS_PER_ITER = 16
_GBUFS = [pltpu.VMEM((N_SLOTS * ROW_WORDS, LANES), I32)] * N_GBUF


PACK_TILE = 512


def _bf16_bits(x):
    b = lax.bitcast_convert_type(x, jnp.uint32)
    return (b + 0x7FFF + ((b >> 16) & 1)) & jnp.uint32(0xFFFF0000)


def _pack_body(t_ref, o_ref, *, rows):
    for s in range(ROW_WORDS):
        lo = _bf16_bits(t_ref[:, (2 * s) * LANES:(2 * s + 1) * LANES])
        hi = _bf16_bits(t_ref[:, (2 * s + 1) * LANES:(2 * s + 2) * LANES])
        o_ref[pl.ds(s, rows, stride=ROW_WORDS), :] = lax.bitcast_convert_type(hi | (lo >> 16), I32)


def _pack_table(tab):
    n, d = tab.shape
    assert d == CHUNKS * LANES
    return pl.pallas_call(
        functools.partial(_pack_body, rows=PACK_TILE),
        grid=(n // PACK_TILE,),
        in_specs=[pl.BlockSpec((PACK_TILE, d), lambda i: (i, 0))],
        out_specs=pl.BlockSpec((PACK_TILE * ROW_WORDS, LANES), lambda i: (i, 0)),
        out_shape=jax.ShapeDtypeStruct((n * ROW_WORDS, LANES), I32),
        compiler_params=_params("parallel"),
    )(tab)


def _stage_rows(eid_ref, t, tab_ref, gbuf):
    for r in range(N_SLOTS):
        off = pl.multiple_of(eid_ref[t, r], ROW_WORDS)
        gbuf[r * ROW_WORDS:(r + 1) * ROW_WORDS, :] = tab_ref[pl.ds(off, ROW_WORDS), :]


def _staged(gbuf):
    return pltpu.bitcast(gbuf[...], BF16)


def _for_each_token(tb, eid_ref, tab_ref, gbufs, start, finish):
    _stage_rows(eid_ref, 0, tab_ref, gbufs[0])

    def group(i, carry):
        t0 = TOKENS_PER_ITER * i
        for j in range(TOKENS_PER_ITER):
            y = start(t0 + j, _staged(gbufs[j % 2]))
            nxt = t0 + j + 1
            if j == TOKENS_PER_ITER - 1:
                nxt = jnp.minimum(nxt, tb - 1)
            _stage_rows(eid_ref, nxt, tab_ref, gbufs[(j + 1) % 2])
            finish(t0 + j, y)
        return carry

    lax.fori_loop(0, tb // TOKENS_PER_ITER, group, 0)


def _chunk_mask():
    k = lax.broadcasted_iota(I32, (CHUNKS, GB_ROWS), 1)
    q = lax.broadcasted_iota(I32, (CHUNKS, GB_ROWS), 0)
    return (k & (CHUNKS - 1)) == q


def _token_rows(t):
    return pl.ds(pl.multiple_of(t * CHUNKS, CHUNKS), CHUNKS)


def _upass_body(eid_ref, xn_ref, gate_ref, tab_ref, rexp_ref, rsum_ref, w_ref, *scratch, tb):
    gbufs, (xbuf, zbuf) = scratch[:N_GBUF], scratch[N_GBUF:]
    for q in range(CHUNKS):
        xbuf[pl.ds(q, tb, stride=CHUNKS), :] = xn_ref[:, q * LANES:(q + 1) * LANES]
    mask = _chunk_mask()

    def start(t, g):
        x8 = xbuf[_token_rows(t), :].astype(BF16)
        return lax.dot_general(x8, g, _NT, preferred_element_type=F32)

    def finish(t, y):
        zbuf[pl.ds(t, 1), :] = jnp.sum(jnp.where(mask, y, 0.0), axis=0, keepdims=True)

    _for_each_token(tb, eid_ref, tab_ref, gbufs, start, finish)
    a = jnp.dot(zbuf[...], rsum_ref[...], preferred_element_type=F32, precision=lax.Precision.HIGHEST)
    act = 0.5 * a * (1.0 + lax.erf(a * math.sqrt(0.5)))
    w = (gate_ref[...] * act).astype(BF16)
    w_ref[...] = jnp.dot(w, rexp_ref[...], preferred_element_type=F32)


def _table_spec(tab):
    return pl.BlockSpec(tab.shape, lambda i: (0, 0), pipeline_mode=pl.Buffered(1))


def _gather_specs(d):
    tb = GATHER_TILE
    smem = pl.BlockSpec((tb, N_SLOTS), lambda i: (i, 0), memory_space=pltpu.SMEM)
    return tb, smem, (lambda c: pl.BlockSpec((tb, c), lambda i: (i, 0)))


def _upass(eid, xn, gate, tab, rexp, rsum):
    n, d = xn.shape
    tb, smem, row = _gather_specs(d)
    return pl.pallas_call(
        functools.partial(_upass_body, tb=tb),
        grid=(n // tb,),
        in_specs=[smem, row(d), row(N_SLOTS), _table_spec(tab), _resident(rexp.shape), _resident(rsum.shape)],
        out_specs=row(GB_ROWS),
        out_shape=jax.ShapeDtypeStruct((n, GB_ROWS), F32),
        scratch_shapes=_GBUFS + [
                        pltpu.VMEM((tb * CHUNKS, LANES), F32),
                        pltpu.VMEM((tb, GB_ROWS), F32)],
        compiler_params=_params("parallel"),
    )(eid, xn, gate, tab, rexp, rsum)


def _vpass_body(eid_ref, w_ref, h_ref, g_ref, tab_ref, y_ref, *scratch, tb):
    gbufs, pbuf = scratch[:N_GBUF], scratch[N_GBUF]
    mask = _chunk_mask()

    def start(t, g):
        wsel = jnp.where(mask, w_ref[pl.ds(t, 1), :], 0.0).astype(BF16)
        return jnp.dot(wsel, g, preferred_element_type=F32)

    def finish(t, y):
        pbuf[_token_rows(t), :] = y

    _for_each_token(tb, eid_ref, tab_ref, gbufs, start, finish)
    peer = jnp.concatenate([pbuf[pl.ds(q, tb, stride=CHUNKS), :] for q in range(CHUNKS)], axis=1)
    y_ref[...] = _rms(h_ref[...] + peer, g_ref[...])


def _vpass(eid, w, h, gfin, tab):
    n, d = h.shape
    tb, smem, row = _gather_specs(d)
    return pl.pallas_call(
        functools.partial(_vpass_body, tb=tb),
        grid=(n // tb,),
        in_specs=[smem, row(GB_ROWS), row(d), _resident((1, d)), _table_spec(tab)],
        out_specs=row(d),
        out_shape=jax.ShapeDtypeStruct((n, d), F32),
        scratch_shapes=_GBUFS + [pltpu.VMEM((tb * CHUNKS, LANES), F32)],
        compiler_params=_params("parallel"),
    )(eid, w, h, gfin, tab)


def _peer_and_final_norm(h, g_ffn, wq_bf, sk_bf, u_tab, v_tab, g_final):
    xn, eid, gate = _route(h, g_ffn, wq_bf, sk_bf)
    slot_of_row = jnp.arange(GB_ROWS, dtype=I32) // CHUNKS
    rexp = (jnp.arange(N_SLOTS, dtype=I32)[:, None] == slot_of_row[None, :])
    w = _upass(eid, xn, gate, u_tab, rexp.astype(BF16), rexp.T.astype(F32))
    return _vpass(eid, w, h, g_final.reshape(1, -1), v_tab)


def kernel(x_prompt, x_sample, cache_k, cache_v, state_pool, page_table, norm_mix_g, w_in, pool_group_w,
           pool_scale, lambda_q1, lambda_k1, lambda_q2, lambda_k2, subln_g, rel_bias, w_branch_pool,
           w_branch_attn, w_out, norm_ffn_g, peer_wq, peer_subkeys, peer_u, peer_v, norm_final_g):
    depth = w_in.shape[0]
    assert depth == 1
    b_p, s_p, d = x_prompt.shape
    b_s, t_s, _ = x_sample.shape
    psz = cache_k.shape[2]
    past = page_table.shape[1] * psz
    width = N_HEADS * V_DIM
    assert ATT_BLOCK >= MAX_DISTANCE and s_p % ATT_BLOCK == 0
    l = 0
    lam_init = 0.8 - 0.6 * math.exp(-0.3 * l)

    w_in_bf = w_in[l].astype(BF16)
    gw_bf = pool_group_w[l].astype(BF16)
    wbp, wba, wout = (w[l].astype(BF16) for w in (w_branch_pool, w_branch_attn, w_out))
    wq_bf = peer_wq[l].astype(BF16)
    sk_bf = peer_subkeys[l].astype(BF16).reshape(2 * PEER_HEADS, N_KEYS, PEER_HALF)
    u_tab = _pack_table(peer_u.reshape(-1, d))
    v_tab = _pack_table(peer_v.reshape(-1, d))
    g_mix = norm_mix_g[l].reshape(1, d)
    g_ffn = norm_ffn_g[l].reshape(1, d)
    scale = pool_scale[l].reshape(1, -1)
    sg = subln_g[l].reshape(1, V_DIM)
    lams = tuple(a[l].astype(F32).reshape(1, HEAD_DIM) for a in (lambda_q1, lambda_k1, lambda_q2, lambda_k2))

    r = jnp.arange(ATT_BLOCK, dtype=I32)
    dist = jnp.stack([dj * ATT_BLOCK + r[:, None] - r[None, :] for dj in range(3)])
    bias_tiles = _rel_bias(rel_bias, dist)

    tq = jnp.arange(t_s, dtype=I32)
    dist_past = past + tq[:, None] - jnp.arange(past, dtype=I32)[None, :]
    rows = 2 * N_HEADS * t_s
    per_map = lambda b: jnp.broadcast_to(b[:, None], (N_HEADS, 2) + b.shape[1:]).reshape(rows, -1)
    step_keys = PAGES_PER_STEP * psz
    bias_past = per_map(_rel_bias(rel_bias, dist_past))
    bias_past = bias_past.reshape(rows, past // step_keys, step_keys).transpose(1, 0, 2)
    dist_new = tq[:, None] - jnp.arange(psz, dtype=I32)[None, :]
    dist_new = jnp.where(jnp.arange(psz)[None, :] < t_s, dist_new, -1)
    bias_new = per_map(_rel_bias(rel_bias, dist_new))

    def token_stages(x, hist, pos0, attention):
        b, length, _ = x.shape
        n = b * length
        u, q, k, v, gp, ga = _inproj(x.reshape(n, d), g_mix, w_in_bf)
        pool_o = _pool(hist, u.reshape(b, length, -1), gw_bf, scale, pos0)
        attn_o = attention(q, k, v)
        h = _merge(x.reshape(n, d), pool_o.reshape(n, -1), attn_o, gp, ga, wbp, wba, wout)
        y = _peer_and_final_norm(h, g_ffn, wq_bf, sk_bf, u_tab, v_tab, norm_final_g)
        return y.reshape(b, length, d), u, k, v

    hist_p = jnp.zeros((b_p, HIST_ROWS, width), F32)
    attn_p = lambda q, k, v: _prompt_attention(q, k, v, bias_tiles, lams, sg, b_p, s_p, lam_init)
    y_p, u_p, k_p, v_p = token_stages(x_prompt, hist_p, 0, attn_p)

    hist_s = jnp.pad(state_pool[l], ((0, 0), (HIST_ROWS - POOL_STATE, 0), (0, 0)))
    ck = cache_k.reshape(-1, psz * N_HEADS, V_DIM)
    cv = cache_v.reshape(-1, psz * N_HEADS, V_DIM)

    def attn_s(q, k, v):
        q4 = q.reshape(b_s, t_s, N_HEADS, V_DIM).transpose(0, 2, 1, 3)
        first = jnp.arange(V_DIM) < HEAD_DIM
        zero = jnp.zeros((), BF16)
        q_heads = jnp.concatenate([jnp.where(first, q4, zero), jnp.where(first, zero, q4)], axis=2)
        pad = ((0, 0), (0, (psz - t_s) * N_HEADS), (0, 0))
        k_new = jnp.pad(k.reshape(b_s, t_s * N_HEADS, V_DIM), pad)
        v_new = jnp.pad(v.reshape(b_s, t_s * N_HEADS, V_DIM), pad)
        o = _sample_attention(page_table, q_heads, bias_past, bias_new, k_new, v_new, ck, cv, lams, sg,
                              t_s, lam_init)
        return o.reshape(b_s * t_s, width)

    y_s, u_s, k_s, v_s = token_stages(x_sample, hist_s, past, attn_s)

    kv_shape = lambda b, length: (1, b, length, N_HEADS, V_DIM)
    u_p3 = u_p.reshape(b_p, s_p, width)
    u_s3 = u_s.reshape(b_s, t_s, width)
    pool_prompt = u_p3[:, -POOL_STATE:][None]
    pool_sample = jnp.concatenate([state_pool[l], u_s3], axis=1)[:, -POOL_STATE:][None]
    return (y_p, y_s, k_p.reshape(kv_shape(b_p, s_p)), v_p.reshape(kv_shape(b_p, s_p)), pool_prompt,
            k_s.reshape(kv_shape(b_s, t_s)), v_s.reshape(kv_shape(b_s, t_s)), pool_sample)
```
